```python
import math
import jax, jax.numpy as jnp
from jax import lax
import numpy as np

D_MODEL = 1024
BATCH = 8
SEQ = 4096
DEPTH = 4

N_MIXERS = 3
N_LAYERS_RG = (DEPTH + 2) // N_MIXERS
N_LAYERS_SSD = (DEPTH + 1) // N_MIXERS
N_LAYERS_S5 = DEPTH // N_MIXERS
RMS_EPS = 1e-6
CONV_WIDTH = 4
D_FF = 4 * D_MODEL

RG_WIDTH = D_MODEL
RG_HEADS = 16
RG_HEAD_DIM = RG_WIDTH // RG_HEADS
RG_C = 8.0

SSD_INNER = 2 * D_MODEL
SSD_HEAD_DIM = 64
SSD_HEADS = SSD_INNER // SSD_HEAD_DIM
SSD_GROUPS = 8
SSD_HPG = SSD_HEADS // SSD_GROUPS
SSD_STATE = 128
SSD_CHUNK = 128
SSD_CONV_DIM = SSD_INNER + 2 * SSD_GROUPS * SSD_STATE
SSD_IN_DIM = SSD_INNER + SSD_CONV_DIM + SSD_HEADS
SSD_NORM_GROUP = SSD_INNER // SSD_GROUPS

S5_WIDTH = D_MODEL
S5_GROUP = 16
S5_GROUPS = S5_WIDTH // S5_GROUP
S5_STATE = 64
S5_EIG_CLIP = -1e-4

kernel_name = "hybrid_rglru_ssd_s5_trunk"

F32 = jnp.float32


def rms_norm(x, g):
    xf = x.astype(F32)
    y = xf * lax.rsqrt(jnp.mean(xf * xf, axis=-1, keepdims=True) + RMS_EPS)
    return (y * g.astype(F32)).astype(x.dtype)


def causal_depthwise_conv(x, w, b):
    ch = x.shape[-1]
    y = lax.conv_general_dilated(
        x, w[:, None, :].astype(x.dtype), window_strides=(1,),
        padding=[(CONV_WIDTH - 1, 0)], dimension_numbers=("NWC", "WIO", "NWC"),
        feature_group_count=ch)
    return y + b.astype(x.dtype)


def _real_combine(e1, e2):
    a1, b1 = e1
    a2, b2 = e2
    return a1 * a2, a2 * b1 + b2


def _complex_combine(e1, e2):
    a1r, a1i, b1r, b1i = e1
    a2r, a2i, b2r, b2i = e2
    return (a1r * a2r - a1i * a2i,
            a1r * a2i + a1i * a2r,
            a2r * b1r - a2i * b1i + b2r,
            a2r * b1i + a2i * b1r + b2i)


def segsum_from_cumsum(cs):
    t = cs.shape[-1]
    diff = cs[..., :, None] - cs[..., None, :]
    mask = jnp.tril(jnp.ones((t, t), dtype=bool))
    return jnp.where(mask, diff, -jnp.inf)


def rglru_mixer(x, w_in, conv_w, conv_b, w_a, b_a, w_x, b_x, lam, w_out):
    bsz, seq, _ = x.shape
    proj = x @ w_in
    gate_branch, u = jnp.split(proj, 2, axis=-1)
    u = causal_depthwise_conv(u, conv_w, conv_b)
    ub = u.reshape(bsz, seq, RG_HEADS, RG_HEAD_DIM)
    r = jax.nn.sigmoid(jnp.einsum("blhi,hij->blhj", ub, w_a).reshape(bsz, seq, RG_WIDTH) + b_a)
    i = jax.nn.sigmoid(jnp.einsum("blhi,hij->blhj", ub, w_x).reshape(bsz, seq, RG_WIDTH) + b_x)
    log_a = (-RG_C * r.astype(F32)) * jax.nn.softplus(-lam.astype(F32))
    a = jnp.exp(log_a)
    mult = jnp.sqrt(-jnp.expm1(2.0 * log_a))
    _, h = lax.associative_scan(_real_combine, (a, mult * (i * u).astype(F32)), axis=1)
    y = h.astype(x.dtype) * jax.nn.gelu(gate_branch)
    return y @ w_out


def gated_group_rms_norm(y, z, g):
    v = (y * jax.nn.silu(z)).astype(F32)
    shp = v.shape
    vg = v.reshape(shp[:-1] + (SSD_GROUPS, SSD_NORM_GROUP))
    vg = vg * lax.rsqrt(jnp.mean(vg * vg, axis=-1, keepdims=True) + RMS_EPS)
    return vg.reshape(shp) * g.astype(F32)


def ssd_mixer(x, w_in, conv_w, conv_b, dt_bias, a_log, d_skip, norm_g, w_out):
    bsz, seq, _ = x.shape
    nc = seq // SSD_CHUNK
    proj = x @ w_in
    z, xbc, dt_raw = jnp.split(proj, [SSD_INNER, SSD_INNER + SSD_CONV_DIM], axis=-1)
    xbc = jax.nn.silu(causal_depthwise_conv(xbc, conv_w, conv_b))
    xs, bm, cm = jnp.split(xbc, [SSD_INNER, SSD_INNER + SSD_GROUPS * SSD_STATE], axis=-1)
    dt = jax.nn.softplus((dt_raw + dt_bias).astype(F32))
    a = -jnp.exp(a_log.astype(F32))
    chunk_shape = (bsz, nc, SSD_CHUNK, SSD_GROUPS, SSD_HPG)
    xh = xs.reshape(chunk_shape + (SSD_HEAD_DIM,))
    dtc = dt.reshape(chunk_shape)
    xdt = xh * dtc[..., None]
    bc = bm.reshape(bsz, nc, SSD_CHUNK, SSD_GROUPS, SSD_STATE)
    cc = cm.reshape(bsz, nc, SSD_CHUNK, SSD_GROUPS, SSD_STATE)
    a_cs = jnp.cumsum(dtc * a.reshape(SSD_GROUPS, SSD_HPG), axis=2)
    lmat = jnp.exp(segsum_from_cumsum(a_cs.transpose(0, 3, 4, 1, 2)))
    cb = jnp.einsum("bclgn,bcsgn->bgcls", cc, bc)
    y_diag = jnp.einsum("bgrcls,bcsgrp->bclgrp", cb[:, :, None] * lmat, xdt)
    decay_states = jnp.exp(a_cs[:, :, -1:] - a_cs)
    states = jnp.einsum("bclgn,bclgrp->bcgrpn", bc, xdt * decay_states[..., None])
    a_tot = jnp.pad(a_cs[:, :, -1].transpose(0, 2, 3, 1), ((0, 0), (0, 0), (0, 0), (1, 0)))
    decay_chunk = jnp.exp(segsum_from_cumsum(jnp.cumsum(a_tot, axis=-1)))
    states_pad = jnp.pad(states, ((0, 0), (1, 0), (0, 0), (0, 0), (0, 0), (0, 0)))
    entering = jnp.einsum("bgrzc,bcgrpn->bzgrpn", decay_chunk, states_pad)[:, :nc]
    y_off = jnp.einsum("bclgn,bcgrpn->bclgrp", cc, entering) * jnp.exp(a_cs)[..., None]
    y = y_diag + y_off + xh * d_skip.reshape(SSD_GROUPS, SSD_HPG, 1)
    y = y.reshape(bsz, seq, SSD_INNER)
    y = gated_group_rms_norm(y, z, norm_g).astype(x.dtype)
    return y @ w_out


def s5_mixer(x, w_in, lam_re, lam_im, log_step, b_re, b_im, c_re, c_im, d_skip, w_out):
    bsz, seq, _ = x.shape
    u = x @ w_in
    lr = jnp.minimum(lam_re.astype(F32), S5_EIG_CLIP)
    li = lam_im.astype(F32)
    step = jnp.exp(log_step.astype(F32))[:, None]
    mag = jnp.exp(lr * step)
    ab_re = mag * jnp.cos(li * step)
    ab_im = mag * jnp.sin(li * step)
    den = lr * lr + li * li
    nr = ab_re - 1.0
    q_re = (nr * lr + ab_im * li) / den
    q_im = (ab_im * lr - nr * li) / den
    bb_re = q_re[..., None] * b_re - q_im[..., None] * b_im
    bb_im = q_re[..., None] * b_im + q_im[..., None] * b_re
    ug = u.reshape(bsz, seq, S5_GROUPS, S5_GROUP).astype(F32)
    bu_re = jnp.einsum("blgc,gpc->blgp", ug, bb_re)
    bu_im = jnp.einsum("blgc,gpc->blgp", ug, bb_im)
    a_re = jnp.broadcast_to(ab_re, (1, seq, S5_GROUPS, S5_STATE))
    a_im = jnp.broadcast_to(ab_im, (1, seq, S5_GROUPS, S5_STATE))
    _, _, h_re, h_im = lax.associative_scan(_complex_combine, (a_re, a_im, bu_re, bu_im), axis=1)
    y = jnp.einsum("blgp,gcp->blgc", h_re, c_re) - jnp.einsum("blgp,gcp->blgc", h_im, c_im)
    y = y.reshape(bsz, seq, S5_WIDTH) + d_skip * ug.reshape(bsz, seq, S5_WIDTH)
    y = jax.nn.gelu(y).astype(x.dtype)
    val, gate = jnp.split(y @ w_out, 2, axis=-1)
    return val * jax.nn.sigmoid(gate)


def sqrelu_mlp(x, w1, w2):
    return jnp.square(jax.nn.relu(x @ w1)) @ w2


def _normal(key, shape, scale):
    return jax.random.normal(key, shape, F32) * scale


def _log_uniform(key, shape, lo, hi):
    return jnp.exp(jax.random.uniform(key, shape, F32, math.log(lo), math.log(hi)))


def setup_inputs(seed: int = 0) -> dict:
    key = jax.random.key(seed)
    ks = list(jax.random.split(key, 40))
    nrg, nssd, ns5 = N_LAYERS_RG, N_LAYERS_SSD, N_LAYERS_S5

    x = _normal(ks[0], (BATCH, SEQ, D_MODEL), 1.0)
    norm_g = 1.0 + _normal(ks[1], (DEPTH, 4, D_MODEL), 0.02)
    mlp_w1 = _normal(ks[2], (DEPTH, D_MODEL, D_FF), D_MODEL ** -0.5)
    mlp_w2 = _normal(ks[3], (DEPTH, D_FF, D_MODEL), D_FF ** -0.5)

    rg_w_in = _normal(ks[4], (nrg, D_MODEL, 2 * RG_WIDTH), D_MODEL ** -0.5)
    rg_conv_w = _normal(ks[5], (nrg, CONV_WIDTH, RG_WIDTH), CONV_WIDTH ** -0.5)
    rg_conv_b = _normal(ks[6], (nrg, RG_WIDTH), 0.01)
    rg_w_a = _normal(ks[7], (nrg, RG_HEADS, RG_HEAD_DIM, RG_HEAD_DIM), RG_HEAD_DIM ** -0.5)
    rg_b_a = _normal(ks[8], (nrg, RG_WIDTH), 0.01)
    rg_w_x = _normal(ks[9], (nrg, RG_HEADS, RG_HEAD_DIM, RG_HEAD_DIM), RG_HEAD_DIM ** -0.5)
    rg_b_x = _normal(ks[10], (nrg, RG_WIDTH), 0.01)
    a_c = jax.random.uniform(ks[11], (nrg, RG_WIDTH), F32, 0.9, 0.999)
    s = a_c ** (1.0 / RG_C)
    rg_lam = jnp.log(s) - jnp.log1p(-s)
    rg_w_out = _normal(ks[12], (nrg, RG_WIDTH, D_MODEL), RG_WIDTH ** -0.5)

    ssd_w_in = _normal(ks[13], (nssd, D_MODEL, SSD_IN_DIM), D_MODEL ** -0.5)
    ssd_conv_w = _normal(ks[14], (nssd, CONV_WIDTH, SSD_CONV_DIM), CONV_WIDTH ** -0.5)
    ssd_conv_b = _normal(ks[15], (nssd, SSD_CONV_DIM), 0.01)
    dt0 = jnp.maximum(_log_uniform(ks[16], (nssd, SSD_HEADS), 1e-3, 1e-1), 1e-4)
    ssd_dt_bias = dt0 + jnp.log(-jnp.expm1(-dt0))
    ssd_a_log = jnp.log(jax.random.uniform(ks[17], (nssd, SSD_HEADS), F32, 1.0, 16.0))
    ssd_d = 1.0 + _normal(ks[18], (nssd, SSD_HEADS), 0.01)
    ssd_norm_g = 1.0 + _normal(ks[19], (nssd, SSD_INNER), 0.02)
    ssd_w_out = _normal(ks[20], (nssd, SSD_INNER, D_MODEL), SSD_INNER ** -0.5)

    s5_w_in = _normal(ks[21], (ns5, D_MODEL, S5_WIDTH), D_MODEL ** -0.5)
    s5_lam_re = -0.5 + _normal(ks[22], (ns5, S5_GROUPS, S5_STATE), 0.01)
    n_idx = jnp.pi * jnp.arange(S5_STATE, dtype=F32)
    s5_lam_im = n_idx + _normal(ks[23], (ns5, S5_GROUPS, S5_STATE), 0.01)
    s5_log_step = jnp.log(_log_uniform(ks[24], (ns5, S5_GROUPS), 1e-3, 1e-1))
    bsc = (2.0 * S5_GROUP) ** -0.5
    s5_b_re = _normal(ks[25], (ns5, S5_GROUPS, S5_STATE, S5_GROUP), bsc)
    s5_b_im = _normal(ks[26], (ns5, S5_GROUPS, S5_STATE, S5_GROUP), bsc)
    csc = (2.0 * S5_STATE) ** -0.5
    s5_c_re = _normal(ks[27], (ns5, S5_GROUPS, S5_GROUP, S5_STATE), csc)
    s5_c_im = _normal(ks[28], (ns5, S5_GROUPS, S5_GROUP, S5_STATE), csc)
    s5_d = _normal(ks[29], (ns5, S5_WIDTH), 1.0)
    s5_w_out = _normal(ks[30], (ns5, S5_WIDTH, 2 * D_MODEL), S5_WIDTH ** -0.5)

    return {
        "x": x, "norm_g": norm_g, "mlp_w1": mlp_w1, "mlp_w2": mlp_w2,
        "rg_w_in": rg_w_in, "rg_conv_w": rg_conv_w, "rg_conv_b": rg_conv_b,
        "rg_w_a": rg_w_a, "rg_b_a": rg_b_a, "rg_w_x": rg_w_x, "rg_b_x": rg_b_x,
        "rg_lam": rg_lam, "rg_w_out": rg_w_out,
        "ssd_w_in": ssd_w_in, "ssd_conv_w": ssd_conv_w, "ssd_conv_b": ssd_conv_b,
        "ssd_dt_bias": ssd_dt_bias, "ssd_a_log": ssd_a_log, "ssd_d": ssd_d,
        "ssd_norm_g": ssd_norm_g, "ssd_w_out": ssd_w_out,
        "s5_w_in": s5_w_in, "s5_lam_re": s5_lam_re, "s5_lam_im": s5_lam_im,
        "s5_log_step": s5_log_step, "s5_b_re": s5_b_re, "s5_b_im": s5_b_im,
        "s5_c_re": s5_c_re, "s5_c_im": s5_c_im, "s5_d": s5_d, "s5_w_out": s5_w_out,
    }


def reference(x, norm_g, mlp_w1, mlp_w2,
              rg_w_in, rg_conv_w, rg_conv_b, rg_w_a, rg_b_a, rg_w_x, rg_b_x, rg_lam, rg_w_out,
              ssd_w_in, ssd_conv_w, ssd_conv_b, ssd_dt_bias, ssd_a_log, ssd_d, ssd_norm_g, ssd_w_out,
              s5_w_in, s5_lam_re, s5_lam_im, s5_log_step, s5_b_re, s5_b_im, s5_c_re, s5_c_im,
              s5_d, s5_w_out):
    for layer in range(DEPTH):
        kind = layer % N_MIXERS
        j = layer // N_MIXERS
        g = norm_g[layer]
        h = rms_norm(x, g[0])
        if kind == 0:
            h = rglru_mixer(h, rg_w_in[j], rg_conv_w[j], rg_conv_b[j], rg_w_a[j], rg_b_a[j],
                            rg_w_x[j], rg_b_x[j], rg_lam[j], rg_w_out[j])
        elif kind == 1:
            h = ssd_mixer(h, ssd_w_in[j], ssd_conv_w[j], ssd_conv_b[j], ssd_dt_bias[j],
                          ssd_a_log[j], ssd_d[j], ssd_norm_g[j], ssd_w_out[j])
        else:
            h = s5_mixer(h, s5_w_in[j], s5_lam_re[j], s5_lam_im[j], s5_log_step[j],
                         s5_b_re[j], s5_b_im[j], s5_c_re[j], s5_c_im[j], s5_d[j], s5_w_out[j])
        x = x + rms_norm(h, g[1])
        h = sqrelu_mlp(rms_norm(x, g[2]), mlp_w1[layer], mlp_w2[layer])
        x = x + rms_norm(h, g[3])
    return x
```

```python
import functools

import jax
import jax.numpy as jnp
from jax import lax
from jax.experimental import pallas as pl
from jax.experimental.pallas import tpu as pltpu

F32 = jnp.float32
BF16 = jnp.bfloat16

RMS_EPS = 1e-6
CONV_WIDTH = 4
RG_C = 8.0
RG_BLOCK = 256
SSD_CHUNK = 128
SSD_HEAD_DIM = 64
SSD_GROUPS = 8
SSD_STATE = 128
S5_EIG_CLIP = -1e-4
S5_SLAB = 256

SUBLANES = 8
VMEM_LIMIT = 56 * 1024 * 1024


def _rms(x, g):
    return x * lax.rsqrt(jnp.mean(x * x, axis=-1, keepdims=True) + RMS_EPS) * g


def _softplus(x):
    return jnp.maximum(x, 0.0) + jnp.log1p(jnp.exp(-jnp.abs(x)))


def _dot(a, b):
    return jnp.dot(a, b, preferred_element_type=F32)


def _const_spec(shape, index):
    return pl.BlockSpec(shape, lambda *_: index, pipeline_mode=pl.Buffered(1))


def _mlp_kernel(x_ref, g_ref, w1_ref, w2_ref, o_ref, *, ff_chunk):
    x = x_ref[...]
    g = g_ref[...]
    xn = _rms(x, g[2:3]).astype(BF16)
    d_ff = w1_ref.shape[1]
    acc = jnp.zeros(x.shape, F32)
    for c in range(d_ff // ff_chunk):
        cols = slice(c * ff_chunk, (c + 1) * ff_chunk)
        h = _dot(xn, w1_ref[:, cols])
        h = jnp.square(jnp.maximum(h, 0.0)).astype(BF16)
        acc = acc + _dot(h, w2_ref[cols, :])
    o_ref[...] = x + _rms(acc, g[3:4])


def _mlp_call(xf, norm_g, w1, w2, layer, *, tm=512, ff_chunk=1024):
    t, d = xf.shape
    d_ff = w1.shape[-1]
    return pl.pallas_call(
        functools.partial(_mlp_kernel, ff_chunk=ff_chunk),
        grid=(t // tm,),
        in_specs=[
            pl.BlockSpec((tm, d), lambda i: (i, 0)),
            _const_spec((None, 4, d), (layer, 0, 0)),
            _const_spec((None, d, d_ff), (layer, 0, 0)),
            _const_spec((None, d_ff, d), (layer, 0, 0)),
        ],
        out_specs=pl.BlockSpec((tm, d), lambda i: (i, 0)),
        out_shape=jax.ShapeDtypeStruct((t, d), F32),
        compiler_params=pltpu.CompilerParams(
            dimension_semantics=("parallel",), vmem_limit_bytes=VMEM_LIMIT),
        name=f"mlp_{layer}",
    )(xf, norm_g, w1, w2)


def _rg_kernel(x_ref, g_ref, win_ref, p_ref, wa_ref, wx_ref, wout_ref, o_ref,
               uext, a_s, b_s, gl_s, hc, *, tl):
    rows = SUBLANES * tl
    hist = SUBLANES * (CONV_WIDTH - 1)
    width = wout_ref.shape[0]

    @pl.when(pl.program_id(0) == 0)
    def _():
        uext[0:hist, :] = jnp.zeros((hist, width), F32)
        hc[...] = jnp.zeros(hc.shape, F32)

    x = x_ref[...]
    g = g_ref[...]
    p = p_ref[...]
    xn = _rms(x, g[0:1]).astype(BF16)
    gl_s[...] = jax.nn.gelu(_dot(xn, win_ref[:, :width]), approximate=True)
    uext[hist:hist + rows, :] = _dot(xn, win_ref[:, width:])

    uc = p[4:5]
    for k in range(CONV_WIDTH):
        uc = uc + p[k:k + 1] * uext[SUBLANES * k:SUBLANES * k + rows, :]
    uext[0:hist, :] = uext[rows:rows + hist, :]

    ucb = uc.astype(BF16)
    ra, rx = [], []
    for s in range(width // RG_BLOCK):
        blk = ucb[:, s * RG_BLOCK:(s + 1) * RG_BLOCK]
        ra.append(_dot(blk, wa_ref[s]))
        rx.append(_dot(blk, wx_ref[s]))
    r = jax.nn.sigmoid(jnp.concatenate(ra, axis=1) + p[5:6])
    ig = jax.nn.sigmoid(jnp.concatenate(rx, axis=1) + p[6:7])
    log_a = (-RG_C * r) * _softplus(-p[7:8])
    a_s[...] = jnp.exp(log_a)
    th = jnp.tanh(log_a)
    b_s[...] = jnp.sqrt(-2.0 * th / (1.0 - th)) * (ig * uc)

    def step(t, h):
        r0 = pl.multiple_of(t * SUBLANES, SUBLANES)
        h = a_s[pl.ds(r0, SUBLANES), :] * h + b_s[pl.ds(r0, SUBLANES), :]
        b_s[pl.ds(r0, SUBLANES), :] = h
        return h

    hc[...] = lax.fori_loop(0, tl, step, hc[...], unroll=4)

    y = (b_s[...] * gl_s[...]).astype(BF16)
    o_ref[...] = x + _rms(_dot(y, wout_ref[...]), g[1:2])


def _rg_call(xf, norm_g, layer, w_in, pvec, wa_bd, wx_bd, w_out, *, tl=64):
    t, d = xf.shape
    width = w_out.shape[0]
    rows = SUBLANES * tl
    hist = SUBLANES * (CONV_WIDTH - 1)
    return pl.pallas_call(
        functools.partial(_rg_kernel, tl=tl),
        grid=(t // rows,),
        in_specs=[
            pl.BlockSpec((rows, d), lambda i: (i, 0)),
            _const_spec((None, 4, d), (layer, 0, 0)),
            _const_spec(w_in.shape, (0, 0)),
            _const_spec(pvec.shape, (0, 0)),
            _const_spec(wa_bd.shape, (0, 0, 0)),
            _const_spec(wx_bd.shape, (0, 0, 0)),
            _const_spec(w_out.shape, (0, 0)),
        ],
        out_specs=pl.BlockSpec((rows, d), lambda i: (i, 0)),
        out_shape=jax.ShapeDtypeStruct((t, d), F32),
        scratch_shapes=[
            pltpu.VMEM((rows + hist, width), F32),
            pltpu.VMEM((rows, width), F32),
            pltpu.VMEM((rows, width), F32),
            pltpu.VMEM((rows, width), F32),
            pltpu.VMEM((SUBLANES, width), F32),
        ],
        compiler_params=pltpu.CompilerParams(
            dimension_semantics=("arbitrary",), vmem_limit_bytes=VMEM_LIMIT),
        name=f"rglru_{layer}",
    )(xf, norm_g, w_in, pvec, wa_bd, wx_bd, w_out)


def _s5_prep_kernel(lre_ref, lim_ref, lstep_ref, bre_ref, bim_ref, ab_ref, bbre_ref, bbim_ref):
    lr = jnp.minimum(lre_ref[...], S5_EIG_CLIP)
    li = lim_ref[...]
    step = jnp.exp(lstep_ref[...])
    mag = jnp.exp(lr * step)
    ab_re = mag * jnp.cos(li * step)
    ab_im = mag * jnp.sin(li * step)
    den = lr * lr + li * li
    nr = ab_re - 1.0
    q_re = (nr * lr + ab_im * li) / den
    q_im = (ab_im * lr - nr * li) / den
    b_re = bre_ref[...]
    b_im = bim_ref[...]
    bbre_ref[...] = q_re * b_re - q_im * b_im
    bbim_ref[...] = q_re * b_im + q_im * b_re
    ab_ref[0:1, :] = ab_re
    ab_ref[1:2, :] = ab_im


def _s5_prep_call(lam_re, lam_im, log_step, b_re, b_im):
    n = lam_re.shape[1]
    c = b_re.shape[0]
    return pl.pallas_call(
        _s5_prep_kernel,
        out_shape=(jax.ShapeDtypeStruct((2, n), F32),
                   jax.ShapeDtypeStruct((c, n), F32),
                   jax.ShapeDtypeStruct((c, n), F32)),
        name="s5_discretize",
    )(lam_re, lam_im, log_step, b_re, b_im)


def _s5_kernel(x_ref, g_ref, win_ref, ab_ref, bre_ref, bim_ref, cre_ref, cim_ref, d_ref,
               wout_ref, o_ref, u_s, hre_s, him_s, y_s, cre_c, cim_c, *, tl):
    rows = SUBLANES * tl
    n_slabs, slab_ch, slab_st = bre_ref.shape
    half = slab_st // 2

    @pl.when(pl.program_id(0) == 0)
    def _():
        cre_c[...] = jnp.zeros(cre_c.shape, F32)
        cim_c[...] = jnp.zeros(cim_c.shape, F32)

    x = x_ref[...]
    g = g_ref[...]
    xn = _rms(x, g[0:1]).astype(BF16)
    u = _dot(xn, win_ref[...])
    u_s[...] = u
    ub = u.astype(BF16)

    for s in range(n_slabs):
        us = ub[:, s * slab_ch:(s + 1) * slab_ch]
        hre_s[...] = _dot(us, bre_ref[s])
        him_s[...] = _dot(us, bim_ref[s])
        for hf in range(2):
            lanes = slice(hf * half, (hf + 1) * half)
            st0 = s * slab_st + hf * half
            ar = jnp.broadcast_to(ab_ref[0:1, st0:st0 + half], (SUBLANES, half))
            ai = jnp.broadcast_to(ab_ref[1:2, st0:st0 + half], (SUBLANES, half))

            def step(t, carry, lanes=lanes, ar=ar, ai=ai):
                hr, hi = carry
                r0 = pl.multiple_of(t * SUBLANES, SUBLANES)
                nhr = ar * hr - ai * hi + hre_s[pl.ds(r0, SUBLANES), lanes]
                nhi = ar * hi + ai * hr + him_s[pl.ds(r0, SUBLANES), lanes]
                hre_s[pl.ds(r0, SUBLANES), lanes] = nhr
                him_s[pl.ds(r0, SUBLANES), lanes] = nhi
                return nhr, nhi

            hr, hi = lax.fori_loop(0, tl, step, (cre_c[s, :, lanes], cim_c[s, :, lanes]),
                                   unroll=2)
            cre_c[s, :, lanes] = hr
            cim_c[s, :, lanes] = hi
        y_s[:, s * slab_ch:(s + 1) * slab_ch] = (
            _dot(hre_s[...].astype(BF16), cre_ref[s]) - _dot(him_s[...].astype(BF16), cim_ref[s]))

    y = y_s[...] + d_ref[...] * u_s[...]
    y = jax.nn.gelu(y, approximate=True).astype(BF16)
    vo = _dot(y, wout_ref[...])
    d_out = vo.shape[1] // 2
    glu = vo[:, :d_out] * jax.nn.sigmoid(vo[:, d_out:])
    o_ref[...] = x + _rms(glu, g[1:2])


def _s5_call(xf, norm_g, layer, w_in, ab, bre_bd, bim_bd, cre_bd, cim_bd, d_skip, w_out, *, tl=64):
    t, d = xf.shape
    width = w_in.shape[1]
    rows = SUBLANES * tl
    n_slabs, _, slab_st = bre_bd.shape
    return pl.pallas_call(
        functools.partial(_s5_kernel, tl=tl),
        grid=(t // rows,),
        in_specs=[
            pl.BlockSpec((rows, d), lambda i: (i, 0)),
            _const_spec((None, 4, d), (layer, 0, 0)),
            _const_spec(w_in.shape, (0, 0)),
            _const_spec(ab.shape, (0, 0)),
            _const_spec(bre_bd.shape, (0, 0, 0)),
            _const_spec(bim_bd.shape, (0, 0, 0)),
            _const_spec(cre_bd.shape, (0, 0, 0)),
            _const_spec(cim_bd.shape, (0, 0, 0)),
            _const_spec(d_skip.shape, (0, 0)),
            _const_spec(w_out.shape, (0, 0)),
        ],
        out_specs=pl.BlockSpec((rows, d), lambda i: (i, 0)),
        out_shape=jax.ShapeDtypeStruct((t, d), F32),
        scratch_shapes=[
            pltpu.VMEM((rows, width), F32),
            pltpu.VMEM((rows, slab_st), F32),
            pltpu.VMEM((rows, slab_st), F32),
            pltpu.VMEM((rows, width), F32),
            pltpu.VMEM((n_slabs, SUBLANES, slab_st), F32),
            pltpu.VMEM((n_slabs, SUBLANES, slab_st), F32),
        ],
        compiler_params=pltpu.CompilerParams(
            dimension_semantics=("arbitrary",), vmem_limit_bytes=VMEM_LIMIT),
        name=f"s5_{layer}",
    )(xf, norm_g, w_in, ab, bre_bd, bim_bd, cre_bd, cim_bd, d_skip, w_out)


def _cumsum_rows(v):
    n = v.shape[0]
    row = lax.broadcasted_iota(jnp.int32, v.shape, 0)
    d = 1
    while d < n:
        v = v + jnp.where(row >= d, pltpu.roll(v, d, 0), 0.0)
        d *= 2
    return v


def _expand_heads(v, h0, n, width):
    return jnp.concatenate(
        [jnp.broadcast_to(v[:, h0 + r:h0 + r + 1], (v.shape[0], width)) for r in range(n)], axis=1)


def _ssd_kernel(x_ref, g_ref, wz_ref, wxbc_ref, wdt_ref, cw_ref, hp_ref, cp_ref, wout_ref, o_ref,
                xe, xc, z_s, y_s, dt_s, st, *, ts):
    inner = wz_ref.shape[1]
    conv_dim = wxbc_ref.shape[1]
    hpg = inner // SSD_HEAD_DIM // SSD_GROUPS
    gw = hpg * SSD_HEAD_DIM
    b_off = inner
    c_off = inner + SSD_GROUPS * SSD_STATE
    pad = SUBLANES
    proj_cols = 1024

    @pl.when(pl.program_id(1) == 0)
    def _():
        xe[0:pad, :] = jnp.zeros((pad, conv_dim), F32)
        st[...] = jnp.zeros(st.shape, F32)

    x = x_ref[...]
    g = g_ref[...]
    xn = _rms(x, g[0:1]).astype(BF16)
    z_s[...] = _dot(xn, wz_ref[...])
    dt_s[...] = _dot(xn, wdt_ref[...])
    for c in range(conv_dim // proj_cols):
        cols = slice(c * proj_cols, (c + 1) * proj_cols)
        xe[pad:pad + ts, cols] = _dot(xn, wxbc_ref[:, cols])
    for c in range(conv_dim // proj_cols):
        cols = slice(c * proj_cols, (c + 1) * proj_cols)
        acc = cw_ref[CONV_WIDTH:CONV_WIDTH + 1, cols]
        for k in range(CONV_WIDTH):
            lo = pad - (CONV_WIDTH - 1) + k
            acc = acc + cw_ref[k:k + 1, cols] * xe[lo:lo + ts, cols]
        xc[:, cols] = acc * jax.nn.sigmoid(acc)
    xe[0:pad, :] = xe[ts:ts + pad, :]

    hp = hp_ref[...]
    cp = cp_ref[...]
    a_neg = -jnp.exp(hp[1:2])
    ii = lax.broadcasted_iota(jnp.int32, (SSD_CHUNK, SSD_CHUNK), 0)
    jj = lax.broadcasted_iota(jnp.int32, (SSD_CHUNK, SSD_CHUNK), 1)
    tri = ii >= jj
    col_head = lax.broadcasted_iota(jnp.int32, (SSD_CHUNK, gw), 1) // SSD_HEAD_DIM
    last = SSD_CHUNK - 1

    for c in range(ts // SSD_CHUNK):
        rs = slice(c * SSD_CHUNK, (c + 1) * SSD_CHUNK)
        dt = _softplus(dt_s[rs, :] + hp[0:1])
        cs = _cumsum_rows(dt * a_neg)
        cs_t = cs.T
        dt_t = dt.T
        e_cs = jnp.exp(cs)
        w_st = dt * jnp.exp(cs[last:last + 1, :] - cs)
        for gi in range(SSD_GROUPS):
            bg = xc[rs, b_off + gi * SSD_STATE:b_off + (gi + 1) * SSD_STATE].astype(BF16)
            cg = xc[rs, c_off + gi * SSD_STATE:c_off + (gi + 1) * SSD_STATE].astype(BF16)
            xg = xc[rs, gi * gw:(gi + 1) * gw]
            xgb = xg.astype(BF16)
            cb = lax.dot_general(cg, bg, (((1,), (1,)), ((), ())), preferred_element_type=F32)
            ms = []
            for r in range(hpg):
                h = gi * hpg + r
                seg = cs[:, h:h + 1] - cs_t[h:h + 1, :]
                lm = jnp.exp(jnp.where(tri, seg, -jnp.inf))
                ms.append((cb * lm * dt_t[h:h + 1, :]).astype(BF16))
            mcat = jnp.concatenate(ms, axis=1)
            bd = jnp.concatenate(
                [jnp.where(col_head == r, xgb, jnp.zeros_like(xgb)) for r in range(hpg)], axis=0)
            y_diag = _dot(mcat, bd)
            eg = _expand_heads(e_cs, gi * hpg, hpg, SSD_HEAD_DIM)
            wg = _expand_heads(w_st, gi * hpg, hpg, SSD_HEAD_DIM)
            state = st[gi]
            y_off = _dot(cg, state.astype(BF16)) * eg
            y_s[rs, gi * gw:(gi + 1) * gw] = y_diag + y_off + xg * cp[0:1, gi * gw:(gi + 1) * gw]
            s_new = lax.dot_general(bg, (xg * wg).astype(BF16), (((0,), (0,)), ((), ())),
                                    preferred_element_type=F32)
            st[gi] = eg[last:last + 1, :] * state + s_new

    z = z_s[...]
    v = y_s[...] * (z * jax.nn.sigmoid(z))
    vs = []
    for gi in range(SSD_GROUPS):
        vg = v[:, gi * gw:(gi + 1) * gw]
        vs.append(vg * lax.rsqrt(jnp.mean(vg * vg, axis=-1, keepdims=True) + RMS_EPS))
    vn = (jnp.concatenate(vs, axis=1) * cp[1:2]).astype(BF16)
    o_ref[...] = x + _rms(_dot(vn, wout_ref[...]), g[1:2])


def _ssd_call(x, norm_g, layer, wz, wxbc, wdt, cw, hp, cp, w_out, *, ts=256):
    b, l, d = x.shape
    inner = wz.shape[1]
    conv_dim = wxbc.shape[1]
    gw = inner // SSD_GROUPS
    return pl.pallas_call(
        functools.partial(_ssd_kernel, ts=ts),
        grid=(b, l // ts),
        in_specs=[
            pl.BlockSpec((None, ts, d), lambda i, j: (i, j, 0)),
            _const_spec((None, 4, d), (layer, 0, 0)),
            _const_spec(wz.shape, (0, 0)),
            _const_spec(wxbc.shape, (0, 0)),
            _const_spec(wdt.shape, (0, 0)),
            _const_spec(cw.shape, (0, 0)),
            _const_spec(hp.shape, (0, 0)),
            _const_spec(cp.shape, (0, 0)),
            _const_spec(w_out.shape, (0, 0)),
        ],
        out_specs=pl.BlockSpec((None, ts, d), lambda i, j: (i, j, 0)),
        out_shape=jax.ShapeDtypeStruct((b, l, d), F32),
        scratch_shapes=[
            pltpu.VMEM((ts + SUBLANES, conv_dim), F32),
            pltpu.VMEM((ts, conv_dim), F32),
            pltpu.VMEM((ts, inner), F32),
            pltpu.VMEM((ts, inner), F32),
            pltpu.VMEM((ts, wdt.shape[1]), F32),
            pltpu.VMEM((SSD_GROUPS, SSD_STATE, gw), F32),
        ],
        compiler_params=pltpu.CompilerParams(
            dimension_semantics=("parallel", "arbitrary"), vmem_limit_bytes=VMEM_LIMIT),
        name=f"ssd_{layer}",
    )(x, norm_g, wz, wxbc, wdt, cw, hp, cp, w_out)


def _block_diag(blocks):
    n, r, c = blocks.shape
    eye = jnp.eye(n, dtype=bool)[:, None, :, None]
    out = jnp.where(eye, blocks[:, :, None, :], jnp.zeros((), blocks.dtype))
    return out.reshape(n * r, n * c)


def _slab_block_diag(blocks, per_slab):
    n, r, c = blocks.shape
    slabs = blocks.reshape(n // per_slab, per_slab, r, c)
    return jax.vmap(_block_diag)(slabs)


def kernel(x, norm_g, mlp_w1, mlp_w2, rg_w_in, rg_conv_w, rg_conv_b, rg_w_a, rg_b_a, rg_w_x, rg_b_x, rg_lam, rg_w_out, ssd_w_in, ssd_conv_w, ssd_conv_b, ssd_dt_bias, ssd_a_log, ssd_d, ssd_norm_g, ssd_w_out, s5_w_in, s5_lam_re, s5_lam_im, s5_log_step, s5_b_re, s5_b_im, s5_c_re, s5_c_im, s5_d, s5_w_out):
    bsz, seq, d = x.shape
    assert bsz == SUBLANES, "time-major mixers put the batch on the 8 sublanes"
    depth = norm_g.shape[0]
    n_mixers = 3
    t = bsz * seq

    w1 = mlp_w1.astype(BF16)
    w2 = mlp_w2.astype(BF16)

    def to_time_major(v):
        return v.transpose(1, 0, 2).reshape(t, d)

    def to_batch_major(v):
        return v.reshape(seq, bsz, d).transpose(1, 0, 2)

    time_major = False
    for layer in range(depth):
        kind = layer % n_mixers
        j = layer // n_mixers
        if kind == 1:
            if time_major:
                x = to_batch_major(x)
                time_major = False
            inner = ssd_w_out.shape[1]
            conv_dim = ssd_conv_w.shape[2]
            heads = ssd_dt_bias.shape[1]
            w_in = ssd_w_in[j].astype(BF16)
            wz = w_in[:, :inner]
            wxbc = w_in[:, inner:inner + conv_dim]
            wdt = jnp.pad(w_in[:, inner + conv_dim:], ((0, 0), (0, 128 - heads)))
            cw = jnp.concatenate(
                [ssd_conv_w[j], ssd_conv_b[j][None], jnp.zeros((3, conv_dim), F32)], axis=0)
            hp = jnp.zeros((SUBLANES, 128), F32)
            hp = hp.at[0, :heads].set(ssd_dt_bias[j]).at[1, :heads].set(ssd_a_log[j])
            cp = jnp.zeros((SUBLANES, inner), F32)
            cp = cp.at[0].set(jnp.repeat(ssd_d[j], SSD_HEAD_DIM)).at[1].set(ssd_norm_g[j])
            x = _ssd_call(x.reshape(bsz, seq, d), norm_g, layer, wz, wxbc, wdt, cw, hp, cp,
                          ssd_w_out[j].astype(BF16))
            x = x.reshape(t, d)
        else:
            if not time_major:
                x = to_time_major(x.reshape(bsz, seq, d))
                time_major = True
            if kind == 0:
                heads = rg_w_a.shape[1]
                per = RG_BLOCK // rg_w_a.shape[2]
                pvec = jnp.concatenate(
                    [rg_conv_w[j], rg_conv_b[j][None], rg_b_a[j][None], rg_b_x[j][None],
                     rg_lam[j][None]], axis=0)
                x = _rg_call(x, norm_g, layer, rg_w_in[j].astype(BF16), pvec,
                             _slab_block_diag(rg_w_a[j].astype(BF16), per),
                             _slab_block_diag(rg_w_x[j].astype(BF16), per),
                             rg_w_out[j].astype(BF16))
            else:
                groups, states, gch = s5_b_re.shape[1:]
                n = groups * states
                per = S5_SLAB // gch
                ab, bb_re, bb_im = _s5_prep_call(
                    s5_lam_re[j].reshape(1, n), s5_lam_im[j].reshape(1, n),
                    jnp.repeat(s5_log_step[j], states).reshape(1, n),
                    s5_b_re[j].transpose(2, 0, 1).reshape(gch, n),
                    s5_b_im[j].transpose(2, 0, 1).reshape(gch, n))

                def b_slabs(bb):
                    blocks = bb.reshape(gch, groups, states).transpose(1, 0, 2)
                    return _slab_block_diag(blocks.astype(BF16), per)

                def c_slabs(cc):
                    return _slab_block_diag(cc.transpose(0, 2, 1).astype(BF16), per)

                x = _s5_call(x, norm_g, layer, s5_w_in[j].astype(BF16), ab,
                             b_slabs(bb_re), b_slabs(bb_im), c_slabs(s5_c_re[j]),
                             c_slabs(s5_c_im[j]), s5_d[j].reshape(1, -1),
                             s5_w_out[j].astype(BF16))
        x = _mlp_call(x, norm_g, w1, w2, layer)

    if time_major:
        x = to_batch_major(x)
    return x.reshape(bsz, seq, d)
```

```python
import functools

import jax
import jax.numpy as jnp
from jax import lax
from jax.experimental import pallas as pl
from jax.experimental.pallas import tpu as pltpu

F32 = jnp.float32
BF16 = jnp.bfloat16

RMS_EPS = 1e-6
LOG2_E = 1.4426950408889634
CONV_WIDTH = 4
RG_C = 8.0
RG_BLOCK = 256
SSD_CHUNK = 128
SSD_HEAD_DIM = 64
SSD_GROUPS = 8
SSD_STATE = 128
SSD_BATCH_BLOCK = 4
S5_EIG_CLIP = -1e-4
S5_SLAB = 256

SUBLANES = 8
VMEM_LIMIT = 56 * 1024 * 1024


def _rms(x, g):
    return x * lax.rsqrt(jnp.mean(x * x, axis=-1, keepdims=True) + RMS_EPS) * g


def _softplus(x):
    return jnp.maximum(x, 0.0) + jnp.log1p(jnp.exp(-jnp.abs(x)))


def _dot(a, b):
    return jnp.dot(a, b, preferred_element_type=F32)


def _const_spec(shape, index):
    return pl.BlockSpec(shape, lambda *_: index, pipeline_mode=pl.Buffered(1))


def _mlp_kernel(x_ref, g_ref, w1_ref, w2_ref, o_ref, *, ff_chunk):
    x = x_ref[...]
    g = g_ref[...]
    xn = _rms(x, g[2:3]).astype(BF16)
    d_ff = w1_ref.shape[1]
    acc = jnp.zeros(x.shape, F32)
    for c in range(d_ff // ff_chunk):
        cols = slice(c * ff_chunk, (c + 1) * ff_chunk)
        h = _dot(xn, w1_ref[:, cols])
        h = jnp.square(jnp.maximum(h, 0.0)).astype(BF16)
        acc = acc + _dot(h, w2_ref[cols, :])
    o_ref[...] = x + _rms(acc, g[3:4])


def _mlp_call(xf, norm_g, w1, w2, layer, *, tm=512, ff_chunk=1024):
    t, d = xf.shape
    d_ff = w1.shape[-1]
    return pl.pallas_call(
        functools.partial(_mlp_kernel, ff_chunk=ff_chunk),
        grid=(t // tm,),
        in_specs=[
            pl.BlockSpec((tm, d), lambda i: (i, 0)),
            _const_spec((None, 4, d), (layer, 0, 0)),
            _const_spec((None, d, d_ff), (layer, 0, 0)),
            _const_spec((None, d_ff, d), (layer, 0, 0)),
        ],
        out_specs=pl.BlockSpec((tm, d), lambda i: (i, 0)),
        out_shape=jax.ShapeDtypeStruct((t, d), F32),
        compiler_params=pltpu.CompilerParams(
            dimension_semantics=("parallel",), vmem_limit_bytes=VMEM_LIMIT),
        name=f"mlp_{layer}",
    )(xf, norm_g, w1, w2)


def _rg_kernel(x_ref, g_ref, win_ref, p_ref, wa_ref, wx_ref, wout_ref, o_ref,
               uext, a_s, b_s, gl_s, hc, *, tl):
    rows = SUBLANES * tl
    hist = SUBLANES * (CONV_WIDTH - 1)
    width = wout_ref.shape[0]

    @pl.when(pl.program_id(0) == 0)
    def _():
        uext[0:hist, :] = jnp.zeros((hist, width), F32)
        hc[...] = jnp.zeros(hc.shape, F32)

    x = x_ref[...]
    g = g_ref[...]
    p = p_ref[...]
    xn = _rms(x, g[0:1]).astype(BF16)
    gl_s[...] = jax.nn.gelu(_dot(xn, win_ref[:, :width]), approximate=True)
    uext[hist:hist + rows, :] = _dot(xn, win_ref[:, width:])

    uc = p[4:5]
    for k in range(CONV_WIDTH):
        uc = uc + p[k:k + 1] * uext[SUBLANES * k:SUBLANES * k + rows, :]
    uext[0:hist, :] = uext[rows:rows + hist, :]

    ucb = uc.astype(BF16)
    ra, rx = [], []
    for s in range(width // RG_BLOCK):
        blk = ucb[:, s * RG_BLOCK:(s + 1) * RG_BLOCK]
        ra.append(_dot(blk, wa_ref[s]))
        rx.append(_dot(blk, wx_ref[s]))
    r = jax.nn.sigmoid(jnp.concatenate(ra, axis=1) + p[5:6])
    ig = jax.nn.sigmoid(jnp.concatenate(rx, axis=1) + p[6:7])
    log_a = (-RG_C * r) * _softplus(-p[7:8])
    a_s[...] = jnp.exp(log_a)
    th = jnp.tanh(log_a)
    b_s[...] = jnp.sqrt(-2.0 * th / (1.0 - th)) * (ig * uc)

    def step(t, h):
        r0 = pl.multiple_of(t * SUBLANES, SUBLANES)
        h = a_s[pl.ds(r0, SUBLANES), :] * h + b_s[pl.ds(r0, SUBLANES), :]
        b_s[pl.ds(r0, SUBLANES), :] = h
        return h

    hc[...] = lax.fori_loop(0, tl, step, hc[...], unroll=4)

    y = (b_s[...] * gl_s[...]).astype(BF16)
    o_ref[...] = x + _rms(_dot(y, wout_ref[...]), g[1:2])


def _rg_call(xf, norm_g, layer, w_in, pvec, wa_bd, wx_bd, w_out, *, tl=64):
    t, d = xf.shape
    width = w_out.shape[0]
    rows = SUBLANES * tl
    hist = SUBLANES * (CONV_WIDTH - 1)
    return pl.pallas_call(
        functools.partial(_rg_kernel, tl=tl),
        grid=(t // rows,),
        in_specs=[
            pl.BlockSpec((rows, d), lambda i: (i, 0)),
            _const_spec((None, 4, d), (layer, 0, 0)),
            _const_spec(w_in.shape, (0, 0)),
            _const_spec(pvec.shape, (0, 0)),
            _const_spec(wa_bd.shape, (0, 0, 0)),
            _const_spec(wx_bd.shape, (0, 0, 0)),
            _const_spec(w_out.shape, (0, 0)),
        ],
        out_specs=pl.BlockSpec((rows, d), lambda i: (i, 0)),
        out_shape=jax.ShapeDtypeStruct((t, d), F32),
        scratch_shapes=[
            pltpu.VMEM((rows + hist, width), F32),
            pltpu.VMEM((rows, width), F32),
            pltpu.VMEM((rows, width), F32),
            pltpu.VMEM((rows, width), F32),
            pltpu.VMEM((SUBLANES, width), F32),
        ],
        compiler_params=pltpu.CompilerParams(
            dimension_semantics=("arbitrary",), vmem_limit_bytes=VMEM_LIMIT),
        name=f"rglru_{layer}",
    )(xf, norm_g, w_in, pvec, wa_bd, wx_bd, w_out)


def _s5_prep_kernel(lre_ref, lim_ref, lstep_ref, bre_ref, bim_ref, ab_ref, bbre_ref, bbim_ref):
    lr = jnp.minimum(lre_ref[...], S5_EIG_CLIP)
    li = lim_ref[...]
    step = jnp.exp(lstep_ref[...])
    mag = jnp.exp(lr * step)
    ab_re = mag * jnp.cos(li * step)
    ab_im = mag * jnp.sin(li * step)
    den = lr * lr + li * li
    nr = ab_re - 1.0
    q_re = (nr * lr + ab_im * li) / den
    q_im = (ab_im * lr - nr * li) / den
    b_re = bre_ref[...]
    b_im = bim_ref[...]
    bbre_ref[...] = q_re * b_re - q_im * b_im
    bbim_ref[...] = q_re * b_im + q_im * b_re
    ab_ref[0:1, :] = ab_re
    ab_ref[1:2, :] = ab_im


def _s5_prep_call(lam_re, lam_im, log_step, b_re, b_im):
    n = lam_re.shape[1]
    c = b_re.shape[0]
    return pl.pallas_call(
        _s5_prep_kernel,
        out_shape=(jax.ShapeDtypeStruct((2, n), F32),
                   jax.ShapeDtypeStruct((c, n), F32),
                   jax.ShapeDtypeStruct((c, n), F32)),
        name="s5_discretize",
    )(lam_re, lam_im, log_step, b_re, b_im)


def _s5_kernel(x_ref, g_ref, win_ref, ab_ref, bre_ref, bim_ref, cre_ref, cim_ref, d_ref,
               wout_ref, o_ref, u_s, hre_s, him_s, y_s, cre_c, cim_c, *, tl):
    rows = SUBLANES * tl
    n_slabs, slab_ch, slab_st = bre_ref.shape
    half = slab_st // 2

    @pl.when(pl.program_id(0) == 0)
    def _():
        cre_c[...] = jnp.zeros(cre_c.shape, F32)
        cim_c[...] = jnp.zeros(cim_c.shape, F32)

    x = x_ref[...]
    g = g_ref[...]
    xn = _rms(x, g[0:1]).astype(BF16)
    u = _dot(xn, win_ref[...])
    u_s[...] = u
    ub = u.astype(BF16)

    for s in range(n_slabs):
        us = ub[:, s * slab_ch:(s + 1) * slab_ch]
        hre_s[...] = _dot(us, bre_ref[s])
        him_s[...] = _dot(us, bim_ref[s])
        for hf in range(2):
            lanes = slice(hf * half, (hf + 1) * half)
            st0 = s * slab_st + hf * half
            ar = jnp.broadcast_to(ab_ref[0:1, st0:st0 + half], (SUBLANES, half))
            ai = jnp.broadcast_to(ab_ref[1:2, st0:st0 + half], (SUBLANES, half))

            def step(t, carry, lanes=lanes, ar=ar, ai=ai):
                hr, hi = carry
                r0 = pl.multiple_of(t * SUBLANES, SUBLANES)
                nhr = ar * hr - ai * hi + hre_s[pl.ds(r0, SUBLANES), lanes]
                nhi = ar * hi + ai * hr + him_s[pl.ds(r0, SUBLANES), lanes]
                hre_s[pl.ds(r0, SUBLANES), lanes] = nhr
                him_s[pl.ds(r0, SUBLANES), lanes] = nhi
                return nhr, nhi

            hr, hi = lax.fori_loop(0, tl, step, (cre_c[s, :, lanes], cim_c[s, :, lanes]),
                                   unroll=2)
            cre_c[s, :, lanes] = hr
            cim_c[s, :, lanes] = hi
        y_s[:, s * slab_ch:(s + 1) * slab_ch] = (
            _dot(hre_s[...].astype(BF16), cre_ref[s]) - _dot(him_s[...].astype(BF16), cim_ref[s]))

    y = y_s[...] + d_ref[...] * u_s[...]
    y = jax.nn.gelu(y, approximate=True).astype(BF16)
    vo = _dot(y, wout_ref[...])
    d_out = vo.shape[1] // 2
    glu = vo[:, :d_out] * jax.nn.sigmoid(vo[:, d_out:])
    o_ref[...] = x + _rms(glu, g[1:2])


def _s5_call(xf, norm_g, layer, w_in, ab, bre_bd, bim_bd, cre_bd, cim_bd, d_skip, w_out, *, tl=64):
    t, d = xf.shape
    width = w_in.shape[1]
    rows = SUBLANES * tl
    n_slabs, _, slab_st = bre_bd.shape
    return pl.pallas_call(
        functools.partial(_s5_kernel, tl=tl),
        grid=(t // rows,),
        in_specs=[
            pl.BlockSpec((rows, d), lambda i: (i, 0)),
            _const_spec((None, 4, d), (layer, 0, 0)),
            _const_spec(w_in.shape, (0, 0)),
            _const_spec(ab.shape, (0, 0)),
            _const_spec(bre_bd.shape, (0, 0, 0)),
            _const_spec(bim_bd.shape, (0, 0, 0)),
            _const_spec(cre_bd.shape, (0, 0, 0)),
            _const_spec(cim_bd.shape, (0, 0, 0)),
            _const_spec(d_skip.shape, (0, 0)),
            _const_spec(w_out.shape, (0, 0)),
        ],
        out_specs=pl.BlockSpec((rows, d), lambda i: (i, 0)),
        out_shape=jax.ShapeDtypeStruct((t, d), F32),
        scratch_shapes=[
            pltpu.VMEM((rows, width), F32),
            pltpu.VMEM((rows, slab_st), F32),
            pltpu.VMEM((rows, slab_st), F32),
            pltpu.VMEM((rows, width), F32),
            pltpu.VMEM((n_slabs, SUBLANES, slab_st), F32),
            pltpu.VMEM((n_slabs, SUBLANES, slab_st), F32),
        ],
        compiler_params=pltpu.CompilerParams(
            dimension_semantics=("arbitrary",), vmem_limit_bytes=VMEM_LIMIT),
        name=f"s5_{layer}",
    )(xf, norm_g, w_in, ab, bre_bd, bim_bd, cre_bd, cim_bd, d_skip, w_out)


def _cumsum_time(v):
    n = v.shape[0]
    d = SUBLANES
    while d < n:
        v = v + jnp.concatenate([jnp.zeros((d, v.shape[1]), v.dtype), v[:n - d]], axis=0)
        d *= 2
    return v


def _ssd_kernel(x_ref, g_ref, wg_ref, wdt_ref, cw_ref, hp_ref, cp_ref, wout_ref, o_ref,
                xn_s, ext, hist, sl, y_sl, csb_s, cst_s, wbt_s, st):
    rows = x_ref.shape[0]
    n_groups = wg_ref.shape[0]
    gw = wout_ref.shape[0] // n_groups
    hpg = gw // SSD_HEAD_DIM
    conv_w = gw + 2 * SSD_STATE
    x_sl = gw // 128
    hist_rows = SUBLANES * (CONV_WIDTH - 1)
    last = SSD_CHUNK - 1

    @pl.when(pl.program_id(0) == 0)
    def _():
        hist[...] = jnp.zeros(hist.shape, F32)
        st[...] = jnp.zeros(st.shape, F32)

    def batch_rows(b):
        return pl.ds(b, SSD_CHUNK, stride=SUBLANES)

    g = g_ref[...]
    hp = hp_ref[...]
    xn_s[...] = _rms(x_ref[...], g[0:1]).astype(BF16)

    dt = _softplus(_dot(xn_s[...], wdt_ref[...]) + hp[0:1])
    cs = _cumsum_time(dt * (-jnp.exp(hp[1:2])))
    cs_end = jnp.broadcast_to(cs[rows - SUBLANES:rows][None], (SSD_CHUNK, SUBLANES, cs.shape[1]))
    cs2 = cs * LOG2_E
    sl[0] = cs2
    sl[1] = cs2 - jnp.log2(dt)
    sl[2] = dt * jnp.exp(cs_end.reshape(cs.shape) - cs)
    for b in range(SUBLANES):
        csb_s[b] = sl[0, batch_rows(b), :]
        cst_s[b] = sl[1, batch_rows(b), :].T
        wbt_s[b] = sl[2, batch_rows(b), :].T

    ii = lax.broadcasted_iota(jnp.int32, (SSD_CHUNK, SSD_CHUNK), 0)
    jj = lax.broadcasted_iota(jnp.int32, (SSD_CHUNK, SSD_CHUNK), 1)
    tri = ii >= jj
    first_head = jj < SSD_HEAD_DIM
    col_head = lax.broadcasted_iota(jnp.int32, (SSD_CHUNK, gw), 1) // SSD_HEAD_DIM

    for gi in range(n_groups):
        pr = _dot(xn_s[...], wg_ref[gi])
        cw = cw_ref[gi]
        ext[0:hist_rows, :] = hist[gi]
        ext[hist_rows:hist_rows + rows, :] = pr[:, :conv_w]
        acc = cw[CONV_WIDTH:CONV_WIDTH + 1]
        for k in range(CONV_WIDTH):
            acc = acc + cw[k:k + 1] * ext[SUBLANES * k:SUBLANES * k + rows, :]
        hist[gi] = ext[rows:rows + hist_rows, :]
        xbc = acc * jax.nn.sigmoid(acc)
        for j in range(conv_w // 128):
            sl[j] = xbc[:, j * 128:(j + 1) * 128]
        z = pr[:, conv_w:conv_w + gw]
        d_skip = cp_ref[0:1, gi * gw:(gi + 1) * gw]

        def per_batch_block(blk, carry, gi=gi, d_skip=d_skip):
            bs = [blk * SSD_BATCH_BLOCK + k for k in range(SSD_BATCH_BLOCK)]
            xg = [jnp.concatenate([sl[j, batch_rows(b), :] for j in range(x_sl)], axis=1)
                  for b in bs]
            bg = [sl[x_sl, batch_rows(b), :] for b in bs]
            cg = [sl[x_sl + 1, batch_rows(b), :].astype(BF16) for b in bs]
            state = [st[b, gi] for b in bs]
            cb = [lax.dot_general(c, bb.astype(BF16), (((1,), (1,)), ((), ())),
                                  preferred_element_type=F32) for c, bb in zip(cg, bg)]
            y_off = [_dot(c, s.astype(BF16)) for c, s in zip(cg, state)]
            cb = [v.astype(BF16) for v in cb]
            bg_t = [bb.T.astype(BF16) for bb in bg]
            lhs, eg = [], []
            for i, b in enumerate(bs):
                cs_b, cst_b, wbt_b = csb_s[b], cst_s[b], wbt_s[b].astype(BF16)
                ms, ws, es = [], [], []
                for r in range(hpg):
                    h = gi * hpg + r
                    cs_col = jnp.broadcast_to(cs_b[:, h:h + 1], (SSD_CHUNK, SSD_CHUNK))
                    lm = jnp.exp2(jnp.where(tri, cs_col - cst_b[h:h + 1, :], -jnp.inf))
                    ms.append(cb[i] * lm.astype(BF16))
                    ws.append(bg_t[i] * wbt_b[h:h + 1, :])
                    es.append(jnp.exp2(cs_col))
                lhs.append(jnp.concatenate(
                    [jnp.concatenate(ms, axis=1), jnp.concatenate(ws, axis=1)], axis=0))
                eg.append(jnp.concatenate(
                    [jnp.where(first_head, es[2 * q], es[2 * q + 1]) for q in range(hpg // 2)],
                    axis=1))
            res = []
            for i in range(SSD_BATCH_BLOCK):
                xgb = xg[i].astype(BF16)
                bd = jnp.concatenate(
                    [jnp.where(col_head == r, xgb, jnp.zeros_like(xgb)) for r in range(hpg)],
                    axis=0)
                res.append(_dot(lhs[i], bd))
            for i, b in enumerate(bs):
                y = res[i][:SSD_CHUNK] + y_off[i] * eg[i] + xg[i] * d_skip
                for j in range(x_sl):
                    y_sl[j, batch_rows(b), :] = y[:, j * 128:(j + 1) * 128]
                st[b, gi] = eg[i][last:last + 1, :] * state[i] + res[i][SSD_CHUNK:]
            return carry

        lax.fori_loop(0, SUBLANES // SSD_BATCH_BLOCK, per_batch_block, 0)

        y = jnp.concatenate([y_sl[j] for j in range(x_sl)], axis=1)
        v = y * (z * jax.nn.sigmoid(z))
        vn = v * lax.rsqrt(jnp.mean(v * v, axis=-1, keepdims=True) + RMS_EPS)
        vn = (vn * cp_ref[1:2, gi * gw:(gi + 1) * gw]).astype(BF16)
        part = _dot(vn, wout_ref[gi * gw:(gi + 1) * gw, :])
        if gi == 0:
            o_ref[...] = part
        else:
            o_ref[...] += part

    o_ref[...] = x_ref[...] + _rms(o_ref[...], g[1:2])


def _ssd_call(xf, norm_g, layer, wg, wdt, cw, hp, cp, w_out):
    t, d = xf.shape
    rows = SUBLANES * SSD_CHUNK
    n_groups = wg.shape[0]
    gw = w_out.shape[0] // n_groups
    conv_w = gw + 2 * SSD_STATE
    hist_rows = SUBLANES * (CONV_WIDTH - 1)
    per_batch = (SUBLANES, SSD_CHUNK, 128)
    return pl.pallas_call(
        _ssd_kernel,
        grid=(t // rows,),
        in_specs=[
            pl.BlockSpec((rows, d), lambda i: (i, 0)),
            _const_spec((None, 4, d), (layer, 0, 0)),
            _const_spec(wg.shape, (0, 0, 0)),
            _const_spec(wdt.shape, (0, 0)),
            _const_spec(cw.shape, (0, 0, 0)),
            _const_spec(hp.shape, (0, 0)),
            _const_spec(cp.shape, (0, 0)),
            _const_spec(w_out.shape, (0, 0)),
        ],
        out_specs=pl.BlockSpec((rows, d), lambda i: (i, 0)),
        out_shape=jax.ShapeDtypeStruct((t, d), F32),
        scratch_shapes=[
            pltpu.VMEM((rows, d), BF16),
            pltpu.VMEM((rows + hist_rows, conv_w), F32),
            pltpu.VMEM((n_groups, hist_rows, conv_w), F32),
            pltpu.VMEM((conv_w // 128, rows, 128), F32),
            pltpu.VMEM((gw // 128, rows, 128), F32),
            pltpu.VMEM(per_batch, F32),
            pltpu.VMEM(per_batch, F32),
            pltpu.VMEM(per_batch, F32),
            pltpu.VMEM((SUBLANES, n_groups, SSD_STATE, gw), F32),
        ],
        compiler_params=pltpu.CompilerParams(
            dimension_semantics=("arbitrary",), vmem_limit_bytes=VMEM_LIMIT),
        name=f"ssd_{layer}",
    )(xf, norm_g, wg, wdt, cw, hp, cp, w_out)


def _block_diag(blocks):
    n, r, c = blocks.shape
    eye = jnp.eye(n, dtype=bool)[:, None, :, None]
    out = jnp.where(eye, blocks[:, :, None, :], jnp.zeros((), blocks.dtype))
    return out.reshape(n * r, n * c)


def _slab_block_diag(blocks, per_slab):
    n, r, c = blocks.shape
    slabs = blocks.reshape(n // per_slab, per_slab, r, c)
    return jax.vmap(_block_diag)(slabs)


def kernel(x, norm_g, mlp_w1, mlp_w2, rg_w_in, rg_conv_w, rg_conv_b, rg_w_a, rg_b_a, rg_w_x, rg_b_x, rg_lam, rg_w_out, ssd_w_in, ssd_conv_w, ssd_conv_b, ssd_dt_bias, ssd_a_log, ssd_d, ssd_norm_g, ssd_w_out, s5_w_in, s5_lam_re, s5_lam_im, s5_log_step, s5_b_re, s5_b_im, s5_c_re, s5_c_im, s5_d, s5_w_out):
    bsz, seq, d = x.shape
    assert bsz == SUBLANES, "time-major mixers put the batch on the 8 sublanes"
    depth = norm_g.shape[0]
    n_mixers = 3
    t = bsz * seq

    w1 = mlp_w1.astype(BF16)
    w2 = mlp_w2.astype(BF16)

    def to_time_major(v):
        return v.transpose(1, 0, 2).reshape(t, d)

    def to_batch_major(v):
        return v.reshape(seq, bsz, d).transpose(1, 0, 2)

    x = to_time_major(x)
    for layer in range(depth):
        kind = layer % n_mixers
        j = layer // n_mixers
        if kind == 1:
            inner = ssd_w_out.shape[1]
            conv_dim = ssd_conv_w.shape[2]
            heads = ssd_dt_bias.shape[1]
            gn = SSD_GROUPS * SSD_STATE

            def by_group(m):
                r = m.shape[0]
                parts = [m[:, :inner].reshape(r, SSD_GROUPS, -1),
                         m[:, inner:inner + gn].reshape(r, SSD_GROUPS, -1),
                         m[:, inner + gn:].reshape(r, SSD_GROUPS, -1)]
                return jnp.concatenate(parts, axis=2).transpose(1, 0, 2)

            w_in = ssd_w_in[j].astype(BF16)
            wz = w_in[:, :inner].reshape(d, SSD_GROUPS, -1).transpose(1, 0, 2)
            wg = jnp.concatenate([by_group(w_in[:, inner:inner + conv_dim]), wz], axis=2)
            wdt = jnp.pad(w_in[:, inner + conv_dim:], ((0, 0), (0, 128 - heads)))
            cw = by_group(jnp.concatenate(
                [ssd_conv_w[j], ssd_conv_b[j][None], jnp.zeros((3, conv_dim), F32)], axis=0))
            hp = jnp.zeros((SUBLANES, 128), F32)
            hp = hp.at[0, :heads].set(ssd_dt_bias[j]).at[1, :heads].set(ssd_a_log[j])
            cp = jnp.zeros((SUBLANES, inner), F32)
            cp = cp.at[0].set(jnp.repeat(ssd_d[j], SSD_HEAD_DIM)).at[1].set(ssd_norm_g[j])
            x = _ssd_call(x, norm_g, layer, wg, wdt, cw, hp, cp, ssd_w_out[j].astype(BF16))
        else:
            if kind == 0:
                heads = rg_w_a.shape[1]
                per = RG_BLOCK // rg_w_a.shape[2]
                pvec = jnp.concatenate(
                    [rg_conv_w[j], rg_conv_b[j][None], rg_b_a[j][None], rg_b_x[j][None],
                     rg_lam[j][None]], axis=0)
                x = _rg_call(x, norm_g, layer, rg_w_in[j].astype(BF16), pvec,
                             _slab_block_diag(rg_w_a[j].astype(BF16), per),
                             _slab_block_diag(rg_w_x[j].astype(BF16), per),
                             rg_w_out[j].astype(BF16))
            else:
                groups, states, gch = s5_b_re.shape[1:]
                n = groups * states
                per = S5_SLAB // gch
                ab, bb_re, bb_im = _s5_prep_call(
                    s5_lam_re[j].reshape(1, n), s5_lam_im[j].reshape(1, n),
                    jnp.repeat(s5_log_step[j], states).reshape(1, n),
                    s5_b_re[j].transpose(2, 0, 1).reshape(gch, n),
                    s5_b_im[j].transpose(2, 0, 1).reshape(gch, n))

                def b_slabs(bb):
                    blocks = bb.reshape(gch, groups, states).transpose(1, 0, 2)
                    return _slab_block_diag(blocks.astype(BF16), per)

                def c_slabs(cc):
                    return _slab_block_diag(cc.transpose(0, 2, 1).astype(BF16), per)

                x = _s5_call(x, norm_g, layer, s5_w_in[j].astype(BF16), ab,
                             b_slabs(bb_re), b_slabs(bb_im), c_slabs(s5_c_re[j]),
                             c_slabs(s5_c_im[j]), s5_d[j].reshape(1, -1),
                             s5_w_out[j].astype(BF16))
        x = _mlp_call(x, norm_g, w1, w2, layer)

    return to_batch_major(x)
```

```python
import functools

import jax
import jax.numpy as jnp
from jax import lax
from jax.experimental import pallas as pl
from jax.experimental.pallas import tpu as pltpu

F32 = jnp.float32
BF16 = jnp.bfloat16

RMS_EPS = 1e-6
LOG2_E = 1.4426950408889634
CONV_WIDTH = 4
RG_C = 8.0
RG_BLOCK = 256
SSD_CHUNK = 128
SSD_HEAD_DIM = 64
SSD_GROUPS = 8
SSD_STATE = 128
SSD_BATCH_BLOCK = 4
S5_EIG_CLIP = -1e-4
S5_SLAB = 256

SUBLANES = 8
LANES = 128
VMEM_LIMIT = 56 * 1024 * 1024


def _rms(x, g):
    return x * lax.rsqrt(jnp.mean(x * x, axis=-1, keepdims=True) + RMS_EPS) * g


def _softplus(x):
    return jnp.maximum(x, 0.0) + jnp.log1p(jnp.exp(-jnp.abs(x)))


def _sigmoid(x):
    return 0.5 * jnp.tanh(0.5 * x) + 0.5


def _dot(a, b):
    return jnp.dot(a, b, preferred_element_type=F32)


def _const_spec(shape, index):
    return pl.BlockSpec(shape, lambda *_: index, pipeline_mode=pl.Buffered(1))


def _batch_rows(b, n):
    return pl.ds(b, n, stride=SUBLANES)


def _to_time_major(x_ref, slabs):
    bsz, n, d = x_ref.shape
    for b in range(bsz):
        for j in range(d // LANES):
            slabs[j, _batch_rows(b, n), :] = x_ref[b, :, j * LANES:(j + 1) * LANES]
    return jnp.concatenate([slabs[j] for j in range(d // LANES)], axis=1)


def _store_batch_major(o_ref, v, slabs):
    bsz, n, d = o_ref.shape
    for j in range(d // LANES):
        slabs[j] = v[:, j * LANES:(j + 1) * LANES]
    for b in range(bsz):
        for j in range(d // LANES):
            o_ref[b, :, j * LANES:(j + 1) * LANES] = slabs[j, _batch_rows(b, n), :]


def _mlp_kernel(x_ref, g_ref, w1_ref, w2_ref, o_ref, *slabs, ff_chunk):
    x = x_ref[...]
    g = g_ref[...]
    xn = _rms(x, g[2:3]).astype(BF16)
    d_ff = w1_ref.shape[1]
    acc = jnp.zeros(x.shape, F32)
    for c in range(d_ff // ff_chunk):
        cols = slice(c * ff_chunk, (c + 1) * ff_chunk)
        h = _dot(xn, w1_ref[:, cols])
        h = jnp.square(jnp.maximum(h, 0.0)).astype(BF16)
        acc = acc + _dot(h, w2_ref[cols, :])
    out = x + _rms(acc, g[3:4])
    if slabs:
        _store_batch_major(o_ref, out, slabs[0])
    else:
        o_ref[...] = out


def _mlp_call(xf, norm_g, w1, w2, layer, *, batch_major_out=False, tm=512, ff_chunk=1024):
    t, d = xf.shape
    d_ff = w1.shape[-1]
    if batch_major_out:
        steps = tm // SUBLANES
        out_spec = pl.BlockSpec((SUBLANES, steps, d), lambda i: (0, i, 0))
        out_shape = jax.ShapeDtypeStruct((SUBLANES, t // SUBLANES, d), F32)
        scratch = [pltpu.VMEM((d // LANES, tm, LANES), F32)]
    else:
        out_spec = pl.BlockSpec((tm, d), lambda i: (i, 0))
        out_shape = jax.ShapeDtypeStruct((t, d), F32)
        scratch = []
    return pl.pallas_call(
        functools.partial(_mlp_kernel, ff_chunk=ff_chunk),
        grid=(t // tm,),
        in_specs=[
            pl.BlockSpec((tm, d), lambda i: (i, 0)),
            _const_spec((None, 4, d), (layer, 0, 0)),
            _const_spec((None, d, d_ff), (layer, 0, 0)),
            _const_spec((None, d_ff, d), (layer, 0, 0)),
        ],
        out_specs=out_spec,
        out_shape=out_shape,
        scratch_shapes=scratch,
        compiler_params=pltpu.CompilerParams(
            dimension_semantics=("parallel",), vmem_limit_bytes=VMEM_LIMIT),
        name=f"mlp_{layer}",
    )(xf, norm_g, w1, w2)


def _rg_kernel(x_ref, g_ref, win_ref, p_ref, wa_ref, wx_ref, wout_ref, o_ref,
               uext, a_s, b_s, gl_s, hc, *slabs, tl):
    rows = SUBLANES * tl
    hist = SUBLANES * (CONV_WIDTH - 1)
    width = wout_ref.shape[0]

    @pl.when(pl.program_id(0) == 0)
    def _():
        uext[0:hist, :] = jnp.zeros((hist, width), F32)
        hc[...] = jnp.zeros(hc.shape, F32)

    x = _to_time_major(x_ref, slabs[0]) if slabs else x_ref[...]
    g = g_ref[...]
    p = p_ref[...]
    xn = _rms(x, g[0:1]).astype(BF16)
    gl_s[...] = jax.nn.gelu(_dot(xn, win_ref[:, :width]), approximate=True)
    uext[hist:hist + rows, :] = _dot(xn, win_ref[:, width:])

    uc = p[4:5]
    for k in range(CONV_WIDTH):
        uc = uc + p[k:k + 1] * uext[SUBLANES * k:SUBLANES * k + rows, :]
    uext[0:hist, :] = uext[rows:rows + hist, :]

    ucb = uc.astype(BF16)
    ra, rx = [], []
    for s in range(width // RG_BLOCK):
        blk = ucb[:, s * RG_BLOCK:(s + 1) * RG_BLOCK]
        ra.append(_dot(blk, wa_ref[s]))
        rx.append(_dot(blk, wx_ref[s]))
    r = _sigmoid(jnp.concatenate(ra, axis=1) + p[5:6])
    ig = _sigmoid(jnp.concatenate(rx, axis=1) + p[6:7])
    a = jnp.exp(r * (-RG_C * _softplus(-p[7:8])))
    a_s[...] = a
    b_s[...] = jnp.sqrt(1.0 - a * a) * (ig * uc)

    def step(t, h):
        r0 = pl.multiple_of(t * SUBLANES, SUBLANES)
        h = a_s[pl.ds(r0, SUBLANES), :] * h + b_s[pl.ds(r0, SUBLANES), :]
        b_s[pl.ds(r0, SUBLANES), :] = h
        return h

    hc[...] = lax.fori_loop(0, tl, step, hc[...], unroll=True)

    y = (b_s[...] * gl_s[...]).astype(BF16)
    o_ref[...] = x + _rms(_dot(y, wout_ref[...]), g[1:2])


def _rg_call(x, norm_g, layer, w_in, pvec, wa_bd, wx_bd, w_out, *, tl=64):
    d = x.shape[-1]
    t = x.size // d
    width = w_out.shape[0]
    rows = SUBLANES * tl
    hist = SUBLANES * (CONV_WIDTH - 1)
    if x.ndim == 3:
        x_spec = pl.BlockSpec((SUBLANES, tl, d), lambda i: (0, i, 0))
        slabs = [pltpu.VMEM((d // LANES, rows, LANES), F32)]
    else:
        x_spec = pl.BlockSpec((rows, d), lambda i: (i, 0))
        slabs = []
    return pl.pallas_call(
        functools.partial(_rg_kernel, tl=tl),
        grid=(t // rows,),
        in_specs=[
            x_spec,
            _const_spec((None, 4, d), (layer, 0, 0)),
            _const_spec(w_in.shape, (0, 0)),
            _const_spec(pvec.shape, (0, 0)),
            _const_spec(wa_bd.shape, (0, 0, 0)),
            _const_spec(wx_bd.shape, (0, 0, 0)),
            _const_spec(w_out.shape, (0, 0)),
        ],
        out_specs=pl.BlockSpec((rows, d), lambda i: (i, 0)),
        out_shape=jax.ShapeDtypeStruct((t, d), F32),
        scratch_shapes=[
            pltpu.VMEM((rows + hist, width), F32),
            pltpu.VMEM((rows, width), F32),
            pltpu.VMEM((rows, width), F32),
            pltpu.VMEM((rows, width), F32),
            pltpu.VMEM((SUBLANES, width), F32),
        ] + slabs,
        compiler_params=pltpu.CompilerParams(
            dimension_semantics=("arbitrary",), vmem_limit_bytes=VMEM_LIMIT),
        name=f"rglru_{layer}",
    )(x, norm_g, w_in, pvec, wa_bd, wx_bd, w_out)


def _s5_prep_kernel(lre_ref, lim_ref, lstep_ref, bre_ref, bim_ref, ab_ref, bbre_ref, bbim_ref):
    lr = jnp.minimum(lre_ref[...], S5_EIG_CLIP)
    li = lim_ref[...]
    step = jnp.exp(lstep_ref[...])
    mag = jnp.exp(lr * step)
    ab_re = mag * jnp.cos(li * step)
    ab_im = mag * jnp.sin(li * step)
    den = lr * lr + li * li
    nr = ab_re - 1.0
    q_re = (nr * lr + ab_im * li) / den
    q_im = (ab_im * lr - nr * li) / den
    b_re = bre_ref[...]
    b_im = bim_ref[...]
    bbre_ref[...] = q_re * b_re - q_im * b_im
    bbim_ref[...] = q_re * b_im + q_im * b_re
    ab_ref[0:1, :] = ab_re
    ab_ref[1:2, :] = ab_im


def _s5_prep_call(lam_re, lam_im, log_step, b_re, b_im):
    n = lam_re.shape[1]
    c = b_re.shape[0]
    return pl.pallas_call(
        _s5_prep_kernel,
        out_shape=(jax.ShapeDtypeStruct((2, n), F32),
                   jax.ShapeDtypeStruct((c, n), F32),
                   jax.ShapeDtypeStruct((c, n), F32)),
        name="s5_discretize",
    )(lam_re, lam_im, log_step, b_re, b_im)


def _s5_kernel(x_ref, g_ref, win_ref, ab_ref, bre_ref, bim_ref, cre_ref, cim_ref, d_ref,
               wout_ref, o_ref, u_s, hre_s, him_s, y_s, cre_c, cim_c, *, tl):
    rows = SUBLANES * tl
    n_slabs, slab_ch, slab_st = bre_ref.shape
    half = slab_st // 2

    @pl.when(pl.program_id(0) == 0)
    def _():
        cre_c[...] = jnp.zeros(cre_c.shape, F32)
        cim_c[...] = jnp.zeros(cim_c.shape, F32)

    x = x_ref[...]
    g = g_ref[...]
    xn = _rms(x, g[0:1]).astype(BF16)
    u = _dot(xn, win_ref[...])
    u_s[...] = u
    ub = u.astype(BF16)

    for s in range(n_slabs):
        us = ub[:, s * slab_ch:(s + 1) * slab_ch]
        hre_s[...] = _dot(us, bre_ref[s])
        him_s[...] = _dot(us, bim_ref[s])
        for hf in range(2):
            lanes = slice(hf * half, (hf + 1) * half)
            st0 = s * slab_st + hf * half
            ar = jnp.broadcast_to(ab_ref[0:1, st0:st0 + half], (SUBLANES, half))
            ai = jnp.broadcast_to(ab_ref[1:2, st0:st0 + half], (SUBLANES, half))

            def step(t, carry, lanes=lanes, ar=ar, ai=ai):
                hr, hi = carry
                r0 = pl.multiple_of(t * SUBLANES, SUBLANES)
                nhr = ar * hr - ai * hi + hre_s[pl.ds(r0, SUBLANES), lanes]
                nhi = ar * hi + ai * hr + him_s[pl.ds(r0, SUBLANES), lanes]
                hre_s[pl.ds(r0, SUBLANES), lanes] = nhr
                him_s[pl.ds(r0, SUBLANES), lanes] = nhi
                return nhr, nhi

            hr, hi = lax.fori_loop(0, tl, step, (cre_c[s, :, lanes], cim_c[s, :, lanes]),
                                   unroll=True)
            cre_c[s, :, lanes] = hr
            cim_c[s, :, lanes] = hi
        y_s[:, s * slab_ch:(s + 1) * slab_ch] = (
            _dot(hre_s[...].astype(BF16), cre_ref[s]) - _dot(him_s[...].astype(BF16), cim_ref[s]))

    y = y_s[...] + d_ref[...] * u_s[...]
    y = jax.nn.gelu(y, approximate=True).astype(BF16)
    vo = _dot(y, wout_ref[...])
    d_out = vo.shape[1] // 2
    glu = vo[:, :d_out] * _sigmoid(vo[:, d_out:])
    o_ref[...] = x + _rms(glu, g[1:2])


def _s5_call(xf, norm_g, layer, w_in, ab, bre_bd, bim_bd, cre_bd, cim_bd, d_skip, w_out, *, tl=64):
    t, d = xf.shape
    width = w_in.shape[1]
    rows = SUBLANES * tl
    n_slabs, _, slab_st = bre_bd.shape
    return pl.pallas_call(
        functools.partial(_s5_kernel, tl=tl),
        grid=(t // rows,),
        in_specs=[
            pl.BlockSpec((rows, d), lambda i: (i, 0)),
            _const_spec((None, 4, d), (layer, 0, 0)),
            _const_spec(w_in.shape, (0, 0)),
            _const_spec(ab.shape, (0, 0)),
            _const_spec(bre_bd.shape, (0, 0, 0)),
            _const_spec(bim_bd.shape, (0, 0, 0)),
            _const_spec(cre_bd.shape, (0, 0, 0)),
            _const_spec(cim_bd.shape, (0, 0, 0)),
            _const_spec(d_skip.shape, (0, 0)),
            _const_spec(w_out.shape, (0, 0)),
        ],
        out_specs=pl.BlockSpec((rows, d), lambda i: (i, 0)),
        out_shape=jax.ShapeDtypeStruct((t, d), F32),
        scratch_shapes=[
            pltpu.VMEM((rows, width), F32),
            pltpu.VMEM((rows, slab_st), F32),
            pltpu.VMEM((rows, slab_st), F32),
            pltpu.VMEM((rows, width), F32),
            pltpu.VMEM((n_slabs, SUBLANES, slab_st), F32),
            pltpu.VMEM((n_slabs, SUBLANES, slab_st), F32),
        ],
        compiler_params=pltpu.CompilerParams(
            dimension_semantics=("arbitrary",), vmem_limit_bytes=VMEM_LIMIT),
        name=f"s5_{layer}",
    )(xf, norm_g, w_in, ab, bre_bd, bim_bd, cre_bd, cim_bd, d_skip, w_out)


def _cumsum_time(v):
    n = v.shape[0]
    d = SUBLANES
    while d < n:
        v = v + jnp.concatenate([jnp.zeros((d, v.shape[1]), v.dtype), v[:n - d]], axis=0)
        d *= 2
    return v


def _ssd_kernel(x_ref, g_ref, wg_ref, wdt_ref, cw_ref, hp_ref, cp_ref, wout_ref, o_ref,
                xn_s, ext, hist, sl, y_sl, csb_s, cst_s, wbt_s, st):
    rows = x_ref.shape[0]
    n_groups = wg_ref.shape[0]
    gw = wout_ref.shape[0] // n_groups
    hpg = gw // SSD_HEAD_DIM
    conv_w = gw + 2 * SSD_STATE
    x_sl = gw // 128
    hist_rows = SUBLANES * (CONV_WIDTH - 1)
    last = SSD_CHUNK - 1

    @pl.when(pl.program_id(0) == 0)
    def _():
        hist[...] = jnp.zeros(hist.shape, F32)
        st[...] = jnp.zeros(st.shape, F32)

    def batch_rows(b):
        return pl.ds(b, SSD_CHUNK, stride=SUBLANES)

    g = g_ref[...]
    hp = hp_ref[...]
    xn_s[...] = _rms(x_ref[...], g[0:1]).astype(BF16)

    dt = _softplus(_dot(xn_s[...], wdt_ref[...]) + hp[0:1])
    cs = _cumsum_time(dt * (-jnp.exp(hp[1:2])))
    cs_end = jnp.broadcast_to(cs[rows - SUBLANES:rows][None], (SSD_CHUNK, SUBLANES, cs.shape[1]))
    cs2 = cs * LOG2_E
    sl[0] = cs2
    sl[1] = cs2 - jnp.log2(dt)
    sl[2] = dt * jnp.exp(cs_end.reshape(cs.shape) - cs)
    for b in range(SUBLANES):
        csb_s[b] = sl[0, batch_rows(b), :]
        cst_s[b] = sl[1, batch_rows(b), :].T
        wbt_s[b] = sl[2, batch_rows(b), :].T

    ii = lax.broadcasted_iota(jnp.int32, (SSD_CHUNK, SSD_CHUNK), 0)
    jj = lax.broadcasted_iota(jnp.int32, (SSD_CHUNK, SSD_CHUNK), 1)
    tri = ii >= jj
    first_head = jj < SSD_HEAD_DIM
    col_head = lax.broadcasted_iota(jnp.int32, (SSD_CHUNK, gw), 1) // SSD_HEAD_DIM

    for gi in range(n_groups):
        pr = _dot(xn_s[...], wg_ref[gi])
        cw = cw_ref[gi]
        ext[0:hist_rows, :] = hist[gi]
        ext[hist_rows:hist_rows + rows, :] = pr[:, :conv_w]
        acc = cw[CONV_WIDTH:CONV_WIDTH + 1]
        for k in range(CONV_WIDTH):
            acc = acc + cw[k:k + 1] * ext[SUBLANES * k:SUBLANES * k + rows, :]
        hist[gi] = ext[rows:rows + hist_rows, :]
        xbc = acc * _sigmoid(acc)
        for j in range(conv_w // 128):
            sl[j] = xbc[:, j * 128:(j + 1) * 128]
        z = pr[:, conv_w:conv_w + gw]
        d_skip = cp_ref[0:1, gi * gw:(gi + 1) * gw]

        def per_batch_block(blk, carry, gi=gi, d_skip=d_skip):
            bs = [blk * SSD_BATCH_BLOCK + k for k in range(SSD_BATCH_BLOCK)]
            xg = [jnp.concatenate([sl[j, batch_rows(b), :] for j in range(x_sl)], axis=1)
                  for b in bs]
            bg = [sl[x_sl, batch_rows(b), :] for b in bs]
            cg = [sl[x_sl + 1, batch_rows(b), :].astype(BF16) for b in bs]
            state = [st[b, gi] for b in bs]
            cb = [lax.dot_general(c, bb.astype(BF16), (((1,), (1,)), ((), ())),
                                  preferred_element_type=F32) for c, bb in zip(cg, bg)]
            y_off = [_dot(c, s.astype(BF16)) for c, s in zip(cg, state)]
            cb = [v.astype(BF16) for v in cb]
            bg_t = [bb.T.astype(BF16) for bb in bg]
            lhs, eg = [], []
            for i, b in enumerate(bs):
                cs_b, cst_b, wbt_b = csb_s[b], cst_s[b], wbt_s[b].astype(BF16)
                ms, ws, es = [], [], []
                for r in range(hpg):
                    h = gi * hpg + r
                    cs_col = jnp.broadcast_to(cs_b[:, h:h + 1], (SSD_CHUNK, SSD_CHUNK))
                    lm = jnp.exp2(jnp.where(tri, cs_col - cst_b[h:h + 1, :], -jnp.inf))
                    ms.append(cb[i] * lm.astype(BF16))
                    ws.append(bg_t[i] * wbt_b[h:h + 1, :])
                    es.append(jnp.exp2(cs_col))
                lhs.append(jnp.concatenate(
                    [jnp.concatenate(ms, axis=1), jnp.concatenate(ws, axis=1)], axis=0))
                eg.append(jnp.concatenate(
                    [jnp.where(first_head, es[2 * q], es[2 * q + 1]) for q in range(hpg // 2)],
                    axis=1))
            res = []
            for i in range(SSD_BATCH_BLOCK):
                xgb = xg[i].astype(BF16)
                bd = jnp.concatenate(
                    [jnp.where(col_head == r, xgb, jnp.zeros_like(xgb)) for r in range(hpg)],
                    axis=0)
                res.append(_dot(lhs[i], bd))
            for i, b in enumerate(bs):
                y = res[i][:SSD_CHUNK] + y_off[i] * eg[i] + xg[i] * d_skip
                for j in range(x_sl):
                    y_sl[j, batch_rows(b), :] = y[:, j * 128:(j + 1) * 128]
                st[b, gi] = eg[i][last:last + 1, :] * state[i] + res[i][SSD_CHUNK:]
            return carry

        lax.fori_loop(0, SUBLANES // SSD_BATCH_BLOCK, per_batch_block, 0)

        y = jnp.concatenate([y_sl[j] for j in range(x_sl)], axis=1)
        v = y * (z * _sigmoid(z))
        vn = v * lax.rsqrt(jnp.mean(v * v, axis=-1, keepdims=True) + RMS_EPS)
        vn = (vn * cp_ref[1:2, gi * gw:(gi + 1) * gw]).astype(BF16)
        part = _dot(vn, wout_ref[gi * gw:(gi + 1) * gw, :])
        if gi == 0:
            o_ref[...] = part
        else:
            o_ref[...] += part

    o_ref[...] = x_ref[...] + _rms(o_ref[...], g[1:2])


def _ssd_call(xf, norm_g, layer, wg, wdt, cw, hp, cp, w_out):
    t, d = xf.shape
    rows = SUBLANES * SSD_CHUNK
    n_groups = wg.shape[0]
    gw = w_out.shape[0] // n_groups
    conv_w = gw + 2 * SSD_STATE
    hist_rows = SUBLANES * (CONV_WIDTH - 1)
    per_batch = (SUBLANES, SSD_CHUNK, 128)
    return pl.pallas_call(
        _ssd_kernel,
        grid=(t // rows,),
        in_specs=[
            pl.BlockSpec((rows, d), lambda i: (i, 0)),
            _const_spec((None, 4, d), (layer, 0, 0)),
            _const_spec(wg.shape, (0, 0, 0)),
            _const_spec(wdt.shape, (0, 0)),
            _const_spec(cw.shape, (0, 0, 0)),
            _const_spec(hp.shape, (0, 0)),
            _const_spec(cp.shape, (0, 0)),
            _const_spec(w_out.shape, (0, 0)),
        ],
        out_specs=pl.BlockSpec((rows, d), lambda i: (i, 0)),
        out_shape=jax.ShapeDtypeStruct((t, d), F32),
        scratch_shapes=[
            pltpu.VMEM((rows, d), BF16),
            pltpu.VMEM((rows + hist_rows, conv_w), F32),
            pltpu.VMEM((n_groups, hist_rows, conv_w), F32),
            pltpu.VMEM((conv_w // 128, rows, 128), F32),
            pltpu.VMEM((gw // 128, rows, 128), F32),
            pltpu.VMEM(per_batch, F32),
            pltpu.VMEM(per_batch, F32),
            pltpu.VMEM(per_batch, F32),
            pltpu.VMEM((SUBLANES, n_groups, SSD_STATE, gw), F32),
        ],
        compiler_params=pltpu.CompilerParams(
            dimension_semantics=("arbitrary",), vmem_limit_bytes=VMEM_LIMIT),
        name=f"ssd_{layer}",
    )(xf, norm_g, wg, wdt, cw, hp, cp, w_out)


def _block_diag(blocks):
    n, r, c = blocks.shape
    eye = jnp.eye(n, dtype=bool)[:, None, :, None]
    out = jnp.where(eye, blocks[:, :, None, :], jnp.zeros((), blocks.dtype))
    return out.reshape(n * r, n * c)


def _slab_block_diag(blocks, per_slab):
    n, r, c = blocks.shape
    slabs = blocks.reshape(n // per_slab, per_slab, r, c)
    return jax.vmap(_block_diag)(slabs)


def kernel(x, norm_g, mlp_w1, mlp_w2, rg_w_in, rg_conv_w, rg_conv_b, rg_w_a, rg_b_a, rg_w_x, rg_b_x, rg_lam, rg_w_out, ssd_w_in, ssd_conv_w, ssd_conv_b, ssd_dt_bias, ssd_a_log, ssd_d, ssd_norm_g, ssd_w_out, s5_w_in, s5_lam_re, s5_lam_im, s5_log_step, s5_b_re, s5_b_im, s5_c_re, s5_c_im, s5_d, s5_w_out):
    bsz, seq, d = x.shape
    assert bsz == SUBLANES, "time-major mixers put the batch on the 8 sublanes"
    depth = norm_g.shape[0]
    n_mixers = 3

    w1 = mlp_w1.astype(BF16)
    w2 = mlp_w2.astype(BF16)

    for layer in range(depth):
        kind = layer % n_mixers
        j = layer // n_mixers
        if kind == 1:
            inner = ssd_w_out.shape[1]
            conv_dim = ssd_conv_w.shape[2]
            heads = ssd_dt_bias.shape[1]
            gn = SSD_GROUPS * SSD_STATE

            def by_group(m):
                r = m.shape[0]
                parts = [m[:, :inner].reshape(r, SSD_GROUPS, -1),
                         m[:, inner:inner + gn].reshape(r, SSD_GROUPS, -1),
                         m[:, inner + gn:].reshape(r, SSD_GROUPS, -1)]
                return jnp.concatenate(parts, axis=2).transpose(1, 0, 2)

            w_in = ssd_w_in[j].astype(BF16)
            wz = w_in[:, :inner].reshape(d, SSD_GROUPS, -1).transpose(1, 0, 2)
            wg = jnp.concatenate([by_group(w_in[:, inner:inner + conv_dim]), wz], axis=2)
            wdt = jnp.pad(w_in[:, inner + conv_dim:], ((0, 0), (0, 128 - heads)))
            cw = by_group(jnp.concatenate(
                [ssd_conv_w[j], ssd_conv_b[j][None], jnp.zeros((3, conv_dim), F32)], axis=0))
            hp = jnp.zeros((SUBLANES, 128), F32)
            hp = hp.at[0, :heads].set(ssd_dt_bias[j]).at[1, :heads].set(ssd_a_log[j])
            cp = jnp.zeros((SUBLANES, inner), F32)
            cp = cp.at[0].set(jnp.repeat(ssd_d[j], SSD_HEAD_DIM)).at[1].set(ssd_norm_g[j])
            x = _ssd_call(x, norm_g, layer, wg, wdt, cw, hp, cp, ssd_w_out[j].astype(BF16))
        else:
            if kind == 0:
                heads = rg_w_a.shape[1]
                per = RG_BLOCK // rg_w_a.shape[2]
                pvec = jnp.concatenate(
                    [rg_conv_w[j], rg_conv_b[j][None], rg_b_a[j][None], rg_b_x[j][None],
                     rg_lam[j][None]], axis=0)
                x = _rg_call(x, norm_g, layer, rg_w_in[j].astype(BF16), pvec,
                             _slab_block_diag(rg_w_a[j].astype(BF16), per),
                             _slab_block_diag(rg_w_x[j].astype(BF16), per),
                             rg_w_out[j].astype(BF16))
            else:
                groups, states, gch = s5_b_re.shape[1:]
                n = groups * states
                per = S5_SLAB // gch
                ab, bb_re, bb_im = _s5_prep_call(
                    s5_lam_re[j].reshape(1, n), s5_lam_im[j].reshape(1, n),
                    jnp.repeat(s5_log_step[j], states).reshape(1, n),
                    s5_b_re[j].transpose(2, 0, 1).reshape(gch, n),
                    s5_b_im[j].transpose(2, 0, 1).reshape(gch, n))

                def b_slabs(bb):
                    blocks = bb.reshape(gch, groups, states).transpose(1, 0, 2)
                    return _slab_block_diag(blocks.astype(BF16), per)

                def c_slabs(cc):
                    return _slab_block_diag(cc.transpose(0, 2, 1).astype(BF16), per)

                x = _s5_call(x, norm_g, layer, s5_w_in[j].astype(BF16), ab,
                             b_slabs(bb_re), b_slabs(bb_im), c_slabs(s5_c_re[j]),
                             c_slabs(s5_c_im[j]), s5_d[j].reshape(1, -1),
                             s5_w_out[j].astype(BF16))
        x = _mlp_call(x, norm_g, w1, w2, layer, batch_major_out=(layer == depth - 1))

    return x
```

```python
import functools

import jax
import jax.numpy as jnp
from jax import lax
from jax.experimental import pallas as pl
from jax.experimental.pallas import tpu as pltpu

F32 = jnp.float32
BF16 = jnp.bfloat16

RMS_EPS = 1e-6
LOG2_E = 1.4426950408889634
CONV_WIDTH = 4
RG_C = 8.0
RG_BLOCK = 256
SSD_CHUNK = 128
SSD_HEAD_DIM = 64
SSD_GROUPS = 8
SSD_STATE = 128
SSD_BATCH_BLOCK = 4
S5_EIG_CLIP = -1e-4
S5_SLAB = 256

SUBLANES = 8
LANES = 128
VMEM_LIMIT = 60 * 1024 * 1024


def _rms(x, g):
    return x * lax.rsqrt(jnp.mean(x * x, axis=-1, keepdims=True) + RMS_EPS) * g


def _softplus(x):
    return jnp.maximum(x, 0.0) + jnp.log1p(jnp.exp(-jnp.abs(x)))


def _sigmoid(x):
    return 0.5 * jnp.tanh(0.5 * x) + 0.5


def _dot(a, b):
    return jnp.dot(a, b, preferred_element_type=F32)


def _const_spec(shape, index):
    return pl.BlockSpec(shape, lambda *_: index, pipeline_mode=pl.Buffered(1))


def _batch_rows(b, n):
    return pl.ds(b, n, stride=SUBLANES)


def _to_time_major(x_ref, slabs):
    bsz, n, d = x_ref.shape
    for b in range(bsz):
        for j in range(d // LANES):
            slabs[j, _batch_rows(b, n), :] = x_ref[b, :, j * LANES:(j + 1) * LANES]
    return jnp.concatenate([slabs[j] for j in range(d // LANES)], axis=1)


def _store_batch_major(o_ref, v, slabs):
    bsz, n, d = o_ref.shape
    for j in range(d // LANES):
        slabs[j] = v[:, j * LANES:(j + 1) * LANES]
    for b in range(bsz):
        for j in range(d // LANES):
            o_ref[b, :, j * LANES:(j + 1) * LANES] = slabs[j, _batch_rows(b, n), :]


def _mlp_kernel(x_ref, g_ref, w1_ref, w2_ref, o_ref, *slabs, ff_chunk):
    x = x_ref[...]
    g = g_ref[...]
    xn = _rms(x, g[2:3]).astype(BF16)
    d_ff = w1_ref.shape[1]
    acc = jnp.zeros(x.shape, F32)
    for c in range(d_ff // ff_chunk):
        cols = slice(c * ff_chunk, (c + 1) * ff_chunk)
        h = _dot(xn, w1_ref[:, cols])
        h = jnp.square(jnp.maximum(h, 0.0)).astype(BF16)
        acc = acc + _dot(h, w2_ref[cols, :])
    out = x + _rms(acc, g[3:4])
    if slabs:
        _store_batch_major(o_ref, out, slabs[0])
    else:
        o_ref[...] = out


def _mlp_call(xf, norm_g, w1, w2, layer, *, batch_major_out=False, tm=1024, ff_chunk=1024):
    t, d = xf.shape
    d_ff = w1.shape[-1]
    if batch_major_out:
        steps = tm // SUBLANES
        out_spec = pl.BlockSpec((SUBLANES, steps, d), lambda i: (0, i, 0))
        out_shape = jax.ShapeDtypeStruct((SUBLANES, t // SUBLANES, d), F32)
        scratch = [pltpu.VMEM((d // LANES, tm, LANES), F32)]
    else:
        out_spec = pl.BlockSpec((tm, d), lambda i: (i, 0))
        out_shape = jax.ShapeDtypeStruct((t, d), F32)
        scratch = []
    return pl.pallas_call(
        functools.partial(_mlp_kernel, ff_chunk=ff_chunk),
        grid=(t // tm,),
        in_specs=[
            pl.BlockSpec((tm, d), lambda i: (i, 0)),
            _const_spec((None, 4, d), (layer, 0, 0)),
            _const_spec((None, d, d_ff), (layer, 0, 0)),
            _const_spec((None, d_ff, d), (layer, 0, 0)),
        ],
        out_specs=out_spec,
        out_shape=out_shape,
        scratch_shapes=scratch,
        compiler_params=pltpu.CompilerParams(
            dimension_semantics=("parallel",), vmem_limit_bytes=VMEM_LIMIT),
        name=f"mlp_{layer}",
    )(xf, norm_g, w1, w2)


def _rg_kernel(x_ref, g_ref, win_ref, p_ref, wa_ref, wx_ref, wout_ref, o_ref,
               uext, a_s, b_s, gl_s, hc, *slabs, tl):
    rows = SUBLANES * tl
    hist = SUBLANES * (CONV_WIDTH - 1)
    width = wout_ref.shape[0]

    @pl.when(pl.program_id(0) == 0)
    def _():
        uext[0:hist, :] = jnp.zeros((hist, width), F32)
        hc[...] = jnp.zeros(hc.shape, F32)

    x = _to_time_major(x_ref, slabs[0]) if slabs else x_ref[...]
    g = g_ref[...]
    p = p_ref[...]
    xn = _rms(x, g[0:1]).astype(BF16)
    gl_s[...] = jax.nn.gelu(_dot(xn, win_ref[:, :width]), approximate=True)
    uext[hist:hist + rows, :] = _dot(xn, win_ref[:, width:])

    uc = p[4:5]
    for k in range(CONV_WIDTH):
        uc = uc + p[k:k + 1] * uext[SUBLANES * k:SUBLANES * k + rows, :]
    uext[0:hist, :] = uext[rows:rows + hist, :]

    ucb = uc.astype(BF16)
    ra, rx = [], []
    for s in range(width // RG_BLOCK):
        blk = ucb[:, s * RG_BLOCK:(s + 1) * RG_BLOCK]
        ra.append(_dot(blk, wa_ref[s]))
        rx.append(_dot(blk, wx_ref[s]))
    r = _sigmoid(jnp.concatenate(ra, axis=1) + p[5:6])
    ig = _sigmoid(jnp.concatenate(rx, axis=1) + p[6:7])
    a = jnp.exp2(r * ((-RG_C * LOG2_E) * _softplus(-p[7:8])))
    a_s[...] = a
    b_s[...] = jnp.sqrt(1.0 - a * a) * (ig * uc)

    def step(t, h):
        r0 = pl.multiple_of(t * SUBLANES, SUBLANES)
        h = a_s[pl.ds(r0, SUBLANES), :] * h + b_s[pl.ds(r0, SUBLANES), :]
        b_s[pl.ds(r0, SUBLANES), :] = h
        return h

    hc[...] = lax.fori_loop(0, tl, step, hc[...], unroll=True)

    y = (b_s[...] * gl_s[...]).astype(BF16)
    o_ref[...] = x + _rms(_dot(y, wout_ref[...]), g[1:2])


def _rg_call(x, norm_g, layer, w_in, pvec, wa_bd, wx_bd, w_out, *, tl=64):
    d = x.shape[-1]
    t = x.size // d
    width = w_out.shape[0]
    rows = SUBLANES * tl
    hist = SUBLANES * (CONV_WIDTH - 1)
    if x.ndim == 3:
        x_spec = pl.BlockSpec((SUBLANES, tl, d), lambda i: (0, i, 0))
        slabs = [pltpu.VMEM((d // LANES, rows, LANES), F32)]
    else:
        x_spec = pl.BlockSpec((rows, d), lambda i: (i, 0))
        slabs = []
    return pl.pallas_call(
        functools.partial(_rg_kernel, tl=tl),
        grid=(t // rows,),
        in_specs=[
            x_spec,
            _const_spec((None, 4, d), (layer, 0, 0)),
            _const_spec(w_in.shape, (0, 0)),
            _const_spec(pvec.shape, (0, 0)),
            _const_spec(wa_bd.shape, (0, 0, 0)),
            _const_spec(wx_bd.shape, (0, 0, 0)),
            _const_spec(w_out.shape, (0, 0)),
        ],
        out_specs=pl.BlockSpec((rows, d), lambda i: (i, 0)),
        out_shape=jax.ShapeDtypeStruct((t, d), F32),
        scratch_shapes=[
            pltpu.VMEM((rows + hist, width), F32),
            pltpu.VMEM((rows, width), F32),
            pltpu.VMEM((rows, width), F32),
            pltpu.VMEM((rows, width), F32),
            pltpu.VMEM((SUBLANES, width), F32),
        ] + slabs,
        compiler_params=pltpu.CompilerParams(
            dimension_semantics=("arbitrary",), vmem_limit_bytes=VMEM_LIMIT),
        name=f"rglru_{layer}",
    )(x, norm_g, w_in, pvec, wa_bd, wx_bd, w_out)


def _s5_prep_kernel(lre_ref, lim_ref, lstep_ref, bre_ref, bim_ref, ab_ref, bbre_ref, bbim_ref):
    lr = jnp.minimum(lre_ref[...], S5_EIG_CLIP)
    li = lim_ref[...]
    step = jnp.exp(lstep_ref[...])
    mag = jnp.exp(lr * step)
    ab_re = mag * jnp.cos(li * step)
    ab_im = mag * jnp.sin(li * step)
    den = lr * lr + li * li
    nr = ab_re - 1.0
    q_re = (nr * lr + ab_im * li) / den
    q_im = (ab_im * lr - nr * li) / den
    b_re = bre_ref[...]
    b_im = bim_ref[...]
    bbre_ref[...] = q_re * b_re - q_im * b_im
    bbim_ref[...] = q_re * b_im + q_im * b_re
    ab_ref[0:1, :] = ab_re
    ab_ref[1:2, :] = ab_im


def _s5_prep_call(lam_re, lam_im, log_step, b_re, b_im):
    n = lam_re.shape[1]
    c = b_re.shape[0]
    return pl.pallas_call(
        _s5_prep_kernel,
        out_shape=(jax.ShapeDtypeStruct((2, n), F32),
                   jax.ShapeDtypeStruct((c, n), F32),
                   jax.ShapeDtypeStruct((c, n), F32)),
        name="s5_discretize",
    )(lam_re, lam_im, log_step, b_re, b_im)


def _s5_kernel(x_ref, g_ref, win_ref, ab_ref, bre_ref, bim_ref, cre_ref, cim_ref, d_ref,
               wout_ref, o_ref, u_s, hre_s, him_s, y_s, cre_c, cim_c, *, tl):
    rows = SUBLANES * tl
    n_slabs, slab_ch, slab_st = bre_ref.shape
    half = slab_st // 2

    @pl.when(pl.program_id(0) == 0)
    def _():
        cre_c[...] = jnp.zeros(cre_c.shape, F32)
        cim_c[...] = jnp.zeros(cim_c.shape, F32)

    x = x_ref[...]
    g = g_ref[...]
    xn = _rms(x, g[0:1]).astype(BF16)
    u = _dot(xn, win_ref[...])
    u_s[...] = u
    ub = u.astype(BF16)

    for s in range(n_slabs):
        us = ub[:, s * slab_ch:(s + 1) * slab_ch]
        hre_s[...] = _dot(us, bre_ref[s])
        him_s[...] = _dot(us, bim_ref[s])
        for hf in range(2):
            lanes = slice(hf * half, (hf + 1) * half)
            st0 = s * slab_st + hf * half
            ar = jnp.broadcast_to(ab_ref[0:1, st0:st0 + half], (SUBLANES, half))
            ai = jnp.broadcast_to(ab_ref[1:2, st0:st0 + half], (SUBLANES, half))

            def step(t, carry, lanes=lanes, ar=ar, ai=ai):
                hr, hi = carry
                r0 = pl.multiple_of(t * SUBLANES, SUBLANES)
                nhr = ar * hr - ai * hi + hre_s[pl.ds(r0, SUBLANES), lanes]
                nhi = ar * hi + ai * hr + him_s[pl.ds(r0, SUBLANES), lanes]
                hre_s[pl.ds(r0, SUBLANES), lanes] = nhr
                him_s[pl.ds(r0, SUBLANES), lanes] = nhi
                return nhr, nhi

            hr, hi = lax.fori_loop(0, tl, step, (cre_c[s, :, lanes], cim_c[s, :, lanes]),
                                   unroll=True)
            cre_c[s, :, lanes] = hr
            cim_c[s, :, lanes] = hi
        y_s[:, s * slab_ch:(s + 1) * slab_ch] = (
            _dot(hre_s[...].astype(BF16), cre_ref[s]) - _dot(him_s[...].astype(BF16), cim_ref[s]))

    y = y_s[...] + d_ref[...] * u_s[...]
    y = jax.nn.gelu(y, approximate=True).astype(BF16)
    vo = _dot(y, wout_ref[...])
    d_out = vo.shape[1] // 2
    glu = vo[:, :d_out] * _sigmoid(vo[:, d_out:])
    o_ref[...] = x + _rms(glu, g[1:2])


def _s5_call(xf, norm_g, layer, w_in, ab, bre_bd, bim_bd, cre_bd, cim_bd, d_skip, w_out, *, tl=64):
    t, d = xf.shape
    width = w_in.shape[1]
    rows = SUBLANES * tl
    n_slabs, _, slab_st = bre_bd.shape
    return pl.pallas_call(
        functools.partial(_s5_kernel, tl=tl),
        grid=(t // rows,),
        in_specs=[
            pl.BlockSpec((rows, d), lambda i: (i, 0)),
            _const_spec((None, 4, d), (layer, 0, 0)),
            _const_spec(w_in.shape, (0, 0)),
            _const_spec(ab.shape, (0, 0)),
            _const_spec(bre_bd.shape, (0, 0, 0)),
            _const_spec(bim_bd.shape, (0, 0, 0)),
            _const_spec(cre_bd.shape, (0, 0, 0)),
            _const_spec(cim_bd.shape, (0, 0, 0)),
            _const_spec(d_skip.shape, (0, 0)),
            _const_spec(w_out.shape, (0, 0)),
        ],
        out_specs=pl.BlockSpec((rows, d), lambda i: (i, 0)),
        out_shape=jax.ShapeDtypeStruct((t, d), F32),
        scratch_shapes=[
            pltpu.VMEM((rows, width), F32),
            pltpu.VMEM((rows, slab_st), F32),
            pltpu.VMEM((rows, slab_st), F32),
            pltpu.VMEM((rows, width), F32),
            pltpu.VMEM((n_slabs, SUBLANES, slab_st), F32),
            pltpu.VMEM((n_slabs, SUBLANES, slab_st), F32),
        ],
        compiler_params=pltpu.CompilerParams(
            dimension_semantics=("arbitrary",), vmem_limit_bytes=VMEM_LIMIT),
        name=f"s5_{layer}",
    )(xf, norm_g, w_in, ab, bre_bd, bim_bd, cre_bd, cim_bd, d_skip, w_out)


def _cumsum_time(v):
    n = v.shape[0]
    d = SUBLANES
    while d < n:
        v = v + jnp.concatenate([jnp.zeros((d, v.shape[1]), v.dtype), v[:n - d]], axis=0)
        d *= 2
    return v


def _ssd_kernel(x_ref, g_ref, wg_ref, wdt_ref, cw_ref, hp_ref, cp_ref, wout_ref, o_ref,
                xn_s, ext, hist, sl, y_sl, csb_s, cst_s, wbt_s, st):
    rows = x_ref.shape[0]
    n_groups = wg_ref.shape[0]
    gw = wout_ref.shape[0] // n_groups
    hpg = gw // SSD_HEAD_DIM
    conv_w = gw + 2 * SSD_STATE
    x_sl = gw // 128
    hist_rows = SUBLANES * (CONV_WIDTH - 1)
    last = SSD_CHUNK - 1

    @pl.when(pl.program_id(0) == 0)
    def _():
        hist[...] = jnp.zeros(hist.shape, F32)
        st[...] = jnp.zeros(st.shape, F32)

    def batch_rows(b):
        return pl.ds(b, SSD_CHUNK, stride=SUBLANES)

    g = g_ref[...]
    hp = hp_ref[...]
    xn_s[...] = _rms(x_ref[...], g[0:1]).astype(BF16)

    dt = _softplus(_dot(xn_s[...], wdt_ref[...]) + hp[0:1])
    cs = _cumsum_time(dt * (-jnp.exp(hp[1:2])))
    cs_end = jnp.broadcast_to(cs[rows - SUBLANES:rows][None], (SSD_CHUNK, SUBLANES, cs.shape[1]))
    cs2 = cs * LOG2_E
    sl[0] = cs2
    sl[1] = cs2 - jnp.log2(dt)
    sl[2] = dt * jnp.exp(cs_end.reshape(cs.shape) - cs)
    for b in range(SUBLANES):
        csb_s[b] = sl[0, batch_rows(b), :]
        cst_s[b] = sl[1, batch_rows(b), :].T
        wbt_s[b] = sl[2, batch_rows(b), :].T

    ii = lax.broadcasted_iota(jnp.int32, (SSD_CHUNK, SSD_CHUNK), 0)
    jj = lax.broadcasted_iota(jnp.int32, (SSD_CHUNK, SSD_CHUNK), 1)
    tri = ii >= jj
    first_head = jj < SSD_HEAD_DIM
    col_head = lax.broadcasted_iota(jnp.int32, (SSD_CHUNK, gw), 1) // SSD_HEAD_DIM

    def conv_to_slabs(gi, pr):
        cw = cw_ref[gi]
        ext[0:hist_rows, :] = hist[gi]
        ext[hist_rows:hist_rows + rows, :] = pr
        acc = cw[CONV_WIDTH:CONV_WIDTH + 1]
        for k in range(CONV_WIDTH):
            acc = acc + cw[k:k + 1] * ext[SUBLANES * k:SUBLANES * k + rows, :]
        hist[gi] = ext[rows:rows + hist_rows, :]
        xbc = acc * _sigmoid(acc)
        for j in range(conv_w // 128):
            sl[j] = xbc[:, j * 128:(j + 1) * 128]

    def gate_and_project_out(gi, z):
        y = jnp.concatenate([y_sl[j] for j in range(x_sl)], axis=1)
        v = y * (z * _sigmoid(z))
        vn = v * lax.rsqrt(jnp.mean(v * v, axis=-1, keepdims=True) + RMS_EPS)
        vn = (vn * cp_ref[1:2, gi * gw:(gi + 1) * gw]).astype(BF16)
        part = _dot(vn, wout_ref[gi * gw:(gi + 1) * gw, :])
        if gi == 0:
            o_ref[...] = part
        else:
            o_ref[...] += part

    for gi in range(n_groups):
        pr = _dot(xn_s[...], wg_ref[gi])
        conv_to_slabs(gi, pr[:, :conv_w])
        z = pr[:, conv_w:conv_w + gw]
        d_skip = cp_ref[0:1, gi * gw:(gi + 1) * gw]

        def per_batch_block(blk, carry, gi=gi, d_skip=d_skip):
            bs = [blk * SSD_BATCH_BLOCK + k for k in range(SSD_BATCH_BLOCK)]
            xg = [jnp.concatenate([sl[j, batch_rows(b), :] for j in range(x_sl)], axis=1)
                  for b in bs]
            bg = [sl[x_sl, batch_rows(b), :] for b in bs]
            cg = [sl[x_sl + 1, batch_rows(b), :].astype(BF16) for b in bs]
            state = [st[b, gi] for b in bs]
            cb = [lax.dot_general(c, bb.astype(BF16), (((1,), (1,)), ((), ())),
                                  preferred_element_type=F32) for c, bb in zip(cg, bg)]
            y_off = [_dot(c, s.astype(BF16)) for c, s in zip(cg, state)]
            cb = [v.astype(BF16) for v in cb]
            bg_t = [bb.T.astype(BF16) for bb in bg]
            lhs, eg = [], []
            for i, b in enumerate(bs):
                cs_b, cst_b, wbt_b = csb_s[b], cst_s[b], wbt_s[b].astype(BF16)
                ms, ws, es = [], [], []
                for r in range(hpg):
                    h = gi * hpg + r
                    cs_col = jnp.broadcast_to(cs_b[:, h:h + 1], (SSD_CHUNK, SSD_CHUNK))
                    lm = jnp.exp2(jnp.where(tri, cs_col - cst_b[h:h + 1, :], -jnp.inf))
                    ms.append(cb[i] * lm.astype(BF16))
                    ws.append(bg_t[i] * wbt_b[h:h + 1, :])
                    es.append(jnp.exp2(cs_col))
                lhs.append(jnp.concatenate(
                    [jnp.concatenate(ms, axis=1), jnp.concatenate(ws, axis=1)], axis=0))
                eg.append(jnp.concatenate(
                    [jnp.where(first_head, es[2 * q], es[2 * q + 1]) for q in range(hpg // 2)],
                    axis=1))
            res = []
            for i in range(SSD_BATCH_BLOCK):
                xgb = xg[i].astype(BF16)
                bd = jnp.concatenate(
                    [jnp.where(col_head == r, xgb, jnp.zeros_like(xgb)) for r in range(hpg)],
                    axis=0)
                res.append(_dot(lhs[i], bd))
            for i, b in enumerate(bs):
                y = res[i][:SSD_CHUNK] + y_off[i] * eg[i] + xg[i] * d_skip
                for j in range(x_sl):
                    y_sl[j, batch_rows(b), :] = y[:, j * 128:(j + 1) * 128]
                st[b, gi] = eg[i][last:last + 1, :] * state[i] + res[i][SSD_CHUNK:]
            return carry

        lax.fori_loop(0, SUBLANES // SSD_BATCH_BLOCK, per_batch_block, 0)
        gate_and_project_out(gi, z)

    o_ref[...] = x_ref[...] + _rms(o_ref[...], g[1:2])


def _ssd_call(xf, norm_g, layer, wg, wdt, cw, hp, cp, w_out):
    t, d = xf.shape
    rows = SUBLANES * SSD_CHUNK
    n_groups = wg.shape[0]
    gw = w_out.shape[0] // n_groups
    conv_w = gw + 2 * SSD_STATE
    hist_rows = SUBLANES * (CONV_WIDTH - 1)
    per_batch = (SUBLANES, SSD_CHUNK, 128)
    return pl.pallas_call(
        _ssd_kernel,
        grid=(t // rows,),
        in_specs=[
            pl.BlockSpec((rows, d), lambda i: (i, 0)),
            _const_spec((None, 4, d), (layer, 0, 0)),
            _const_spec(wg.shape, (0, 0, 0)),
            _const_spec(wdt.shape, (0, 0)),
            _const_spec(cw.shape, (0, 0, 0)),
            _const_spec(hp.shape, (0, 0)),
            _const_spec(cp.shape, (0, 0)),
            _const_spec(w_out.shape, (0, 0)),
        ],
        out_specs=pl.BlockSpec((rows, d), lambda i: (i, 0)),
        out_shape=jax.ShapeDtypeStruct((t, d), F32),
        scratch_shapes=[
            pltpu.VMEM((rows, d), BF16),
            pltpu.VMEM((rows + hist_rows, conv_w), F32),
            pltpu.VMEM((n_groups, hist_rows, conv_w), F32),
            pltpu.VMEM((conv_w // 128, rows, 128), F32),
            pltpu.VMEM((gw // 128, rows, 128), F32),
            pltpu.VMEM(per_batch, F32),
            pltpu.VMEM(per_batch, F32),
            pltpu.VMEM(per_batch, F32),
            pltpu.VMEM((SUBLANES, n_groups, SSD_STATE, gw), F32),
        ],
        compiler_params=pltpu.CompilerParams(
            dimension_semantics=("arbitrary",), vmem_limit_bytes=VMEM_LIMIT),
        name=f"ssd_{layer}",
    )(xf, norm_g, wg, wdt, cw, hp, cp, w_out)


def _block_diag(blocks):
    n, r, c = blocks.shape
    eye = jnp.eye(n, dtype=bool)[:, None, :, None]
    out = jnp.where(eye, blocks[:, :, None, :], jnp.zeros((), blocks.dtype))
    return out.reshape(n * r, n * c)


def _slab_block_diag(blocks, per_slab):
    n, r, c = blocks.shape
    slabs = blocks.reshape(n // per_slab, per_slab, r, c)
    return jax.vmap(_block_diag)(slabs)


def kernel(x, norm_g, mlp_w1, mlp_w2, rg_w_in, rg_conv_w, rg_conv_b, rg_w_a, rg_b_a, rg_w_x, rg_b_x, rg_lam, rg_w_out, ssd_w_in, ssd_conv_w, ssd_conv_b, ssd_dt_bias, ssd_a_log, ssd_d, ssd_norm_g, ssd_w_out, s5_w_in, s5_lam_re, s5_lam_im, s5_log_step, s5_b_re, s5_b_im, s5_c_re, s5_c_im, s5_d, s5_w_out):
    bsz, seq, d = x.shape
    assert bsz == SUBLANES, "time-major mixers put the batch on the 8 sublanes"
    depth = norm_g.shape[0]
    n_mixers = 3

    w1 = mlp_w1.astype(BF16)
    w2 = mlp_w2.astype(BF16)

    for layer in range(depth):
        kind = layer % n_mixers
        j = layer // n_mixers
        if kind == 1:
            inner = ssd_w_out.shape[1]
            conv_dim = ssd_conv_w.shape[2]
            heads = ssd_dt_bias.shape[1]
            gn = SSD_GROUPS * SSD_STATE

            def by_group(m):
                r = m.shape[0]
                parts = [m[:, :inner].reshape(r, SSD_GROUPS, -1),
                         m[:, inner:inner + gn].reshape(r, SSD_GROUPS, -1),
                         m[:, inner + gn:].reshape(r, SSD_GROUPS, -1)]
                return jnp.concatenate(parts, axis=2).transpose(1, 0, 2)

            w_in = ssd_w_in[j].astype(BF16)
            wz = w_in[:, :inner].reshape(d, SSD_GROUPS, -1).transpose(1, 0, 2)
            wg = jnp.concatenate([by_group(w_in[:, inner:inner + conv_dim]), wz], axis=2)
            wdt = jnp.pad(w_in[:, inner + conv_dim:], ((0, 0), (0, 128 - heads)))
            cw = by_group(jnp.concatenate(
                [ssd_conv_w[j], ssd_conv_b[j][None], jnp.zeros((3, conv_dim), F32)], axis=0))
            hp = jnp.zeros((SUBLANES, 128), F32)
            hp = hp.at[0, :heads].set(ssd_dt_bias[j]).at[1, :heads].set(ssd_a_log[j])
            cp = jnp.zeros((SUBLANES, inner), F32)
            cp = cp.at[0].set(jnp.repeat(ssd_d[j], SSD_HEAD_DIM)).at[1].set(ssd_norm_g[j])
            x = _ssd_call(x, norm_g, layer, wg, wdt, cw, hp, cp, ssd_w_out[j].astype(BF16))
        else:
            if kind == 0:
                heads = rg_w_a.shape[1]
                per = RG_BLOCK // rg_w_a.shape[2]
                pvec = jnp.concatenate(
                    [rg_conv_w[j], rg_conv_b[j][None], rg_b_a[j][None], rg_b_x[j][None],
                     rg_lam[j][None]], axis=0)
                x = _rg_call(x, norm_g, layer, rg_w_in[j].astype(BF16), pvec,
                             _slab_block_diag(rg_w_a[j].astype(BF16), per),
                             _slab_block_diag(rg_w_x[j].astype(BF16), per),
                             rg_w_out[j].astype(BF16))
            else:
                groups, states, gch = s5_b_re.shape[1:]
                n = groups * states
                per = S5_SLAB // gch
                ab, bb_re, bb_im = _s5_prep_call(
                    s5_lam_re[j].reshape(1, n), s5_lam_im[j].reshape(1, n),
                    jnp.repeat(s5_log_step[j], states).reshape(1, n),
                    s5_b_re[j].transpose(2, 0, 1).reshape(gch, n),
                    s5_b_im[j].transpose(2, 0, 1).reshape(gch, n))

                def b_slabs(bb):
                    blocks = bb.reshape(gch, groups, states).transpose(1, 0, 2)
                    return _slab_block_diag(blocks.astype(BF16), per)

                def c_slabs(cc):
                    return _slab_block_diag(cc.transpose(0, 2, 1).astype(BF16), per)

                x = _s5_call(x, norm_g, layer, s5_w_in[j].astype(BF16), ab,
                             b_slabs(bb_re), b_slabs(bb_im), c_slabs(s5_c_re[j]),
                             c_slabs(s5_c_im[j]), s5_d[j].reshape(1, -1),
                             s5_w_out[j].astype(BF16))
        x = _mlp_call(x, norm_g, w1, w2, layer, batch_major_out=(layer == depth - 1))

    return x
```

```python
import functools

import jax
import jax.numpy as jnp
from jax import lax
from jax.experimental import pallas as pl
from jax.experimental.pallas import tpu as pltpu

F32 = jnp.float32
BF16 = jnp.bfloat16

RMS_EPS = 1e-6
LOG2_E = 1.4426950408889634
CONV_WIDTH = 4
RG_C = 8.0
RG_BLOCK = 256
SSD_CHUNK = 128
SSD_HEAD_DIM = 64
SSD_GROUPS = 8
SSD_STATE = 128
SSD_BATCH_BLOCK = 4
S5_EIG_CLIP = -1e-4
S5_SLAB = 256

SUBLANES = 8
LANES = 128
VMEM_LIMIT = 60 * 1024 * 1024


def _rms(x, g):
    return x * lax.rsqrt(jnp.mean(x * x, axis=-1, keepdims=True) + RMS_EPS) * g


def _softplus(x):
    return jnp.maximum(x, 0.0) + jnp.log1p(jnp.exp(-jnp.abs(x)))


def _sigmoid(x):
    return 0.5 * jnp.tanh(0.5 * x) + 0.5


def _dot(a, b):
    return jnp.dot(a, b, preferred_element_type=F32)


def _const_spec(shape, index):
    return pl.BlockSpec(shape, lambda *_: index, pipeline_mode=pl.Buffered(1))


def _batch_rows(b, n):
    return pl.ds(b, n, stride=SUBLANES)


def _to_time_major(x_ref, slabs):
    bsz, n, d = x_ref.shape
    for b in range(bsz):
        for j in range(d // LANES):
            slabs[j, _batch_rows(b, n), :] = x_ref[b, :, j * LANES:(j + 1) * LANES]
    return jnp.concatenate([slabs[j] for j in range(d // LANES)], axis=1)


def _store_batch_major(o_ref, v, slabs):
    bsz, n, d = o_ref.shape
    for j in range(d // LANES):
        slabs[j] = v[:, j * LANES:(j + 1) * LANES]
    for b in range(bsz):
        for j in range(d // LANES):
            o_ref[b, :, j * LANES:(j + 1) * LANES] = slabs[j, _batch_rows(b, n), :]


def _mlp_kernel(x_ref, g_ref, w1_ref, w2_ref, o_ref, *slabs, ff_chunk):
    x = x_ref[...]
    g = g_ref[...]
    xn = _rms(x, g[2:3]).astype(BF16)
    d_ff = w1_ref.shape[1]
    acc = jnp.zeros(x.shape, F32)
    for c in range(d_ff // ff_chunk):
        cols = slice(c * ff_chunk, (c + 1) * ff_chunk)
        h = _dot(xn, w1_ref[:, cols])
        h = jnp.square(jnp.maximum(h, 0.0)).astype(BF16)
        acc = acc + _dot(h, w2_ref[cols, :])
    out = x + _rms(acc, g[3:4])
    if slabs:
        _store_batch_major(o_ref, out, slabs[0])
    else:
        o_ref[...] = out


def _mlp_call(xf, norm_g, w1, w2, layer, *, batch_major_out=False, tm=1024, ff_chunk=1024):
    t, d = xf.shape
    d_ff = w1.shape[-1]
    if batch_major_out:
        steps = tm // SUBLANES
        out_spec = pl.BlockSpec((SUBLANES, steps, d), lambda i: (0, i, 0))
        out_shape = jax.ShapeDtypeStruct((SUBLANES, t // SUBLANES, d), F32)
        scratch = [pltpu.VMEM((d // LANES, tm, LANES), F32)]
    else:
        out_spec = pl.BlockSpec((tm, d), lambda i: (i, 0))
        out_shape = jax.ShapeDtypeStruct((t, d), F32)
        scratch = []
    return pl.pallas_call(
        functools.partial(_mlp_kernel, ff_chunk=ff_chunk),
        grid=(t // tm,),
        in_specs=[
            pl.BlockSpec((tm, d), lambda i: (i, 0)),
            _const_spec((None, 4, d), (layer, 0, 0)),
            _const_spec((None, d, d_ff), (layer, 0, 0)),
            _const_spec((None, d_ff, d), (layer, 0, 0)),
        ],
        out_specs=out_spec,
        out_shape=out_shape,
        scratch_shapes=scratch,
        compiler_params=pltpu.CompilerParams(
            dimension_semantics=("parallel",), vmem_limit_bytes=VMEM_LIMIT),
        name=f"mlp_{layer}",
    )(xf, norm_g, w1, w2)


def _rg_kernel(x_ref, g_ref, win_ref, p_ref, wa_ref, wx_ref, wout_ref, o_ref,
               uext, a_s, b_s, gl_s, hc, *slabs, tl):
    rows = SUBLANES * tl
    hist = SUBLANES * (CONV_WIDTH - 1)
    width = wout_ref.shape[0]

    @pl.when(pl.program_id(0) == 0)
    def _():
        uext[0:hist, :] = jnp.zeros((hist, width), F32)
        hc[...] = jnp.zeros(hc.shape, F32)

    x = _to_time_major(x_ref, slabs[0]) if slabs else x_ref[...]
    g = g_ref[...]
    p = p_ref[...]
    xn = _rms(x, g[0:1]).astype(BF16)
    gl_s[...] = jax.nn.gelu(_dot(xn, win_ref[:, :width]), approximate=True)
    uext[hist:hist + rows, :] = _dot(xn, win_ref[:, width:])

    uc = p[4:5]
    for k in range(CONV_WIDTH):
        uc = uc + p[k:k + 1] * uext[SUBLANES * k:SUBLANES * k + rows, :]
    uext[0:hist, :] = uext[rows:rows + hist, :]

    ucb = uc.astype(BF16)
    ra, rx = [], []
    for s in range(width // RG_BLOCK):
        blk = ucb[:, s * RG_BLOCK:(s + 1) * RG_BLOCK]
        ra.append(_dot(blk, wa_ref[s]))
        rx.append(_dot(blk, wx_ref[s]))
    r = _sigmoid(jnp.concatenate(ra, axis=1) + p[5:6])
    ig = _sigmoid(jnp.concatenate(rx, axis=1) + p[6:7])
    a = jnp.exp2(r * ((-RG_C * LOG2_E) * _softplus(-p[7:8])))
    a_s[...] = a
    b_s[...] = jnp.sqrt(1.0 - a * a) * (ig * uc)

    def step(t, h):
        r0 = pl.multiple_of(t * SUBLANES, SUBLANES)
        h = a_s[pl.ds(r0, SUBLANES), :] * h + b_s[pl.ds(r0, SUBLANES), :]
        b_s[pl.ds(r0, SUBLANES), :] = h
        return h

    hc[...] = lax.fori_loop(0, tl, step, hc[...], unroll=True)

    y = (b_s[...] * gl_s[...]).astype(BF16)
    o_ref[...] = x + _rms(_dot(y, wout_ref[...]), g[1:2])


def _rg_call(x, norm_g, layer, w_in, pvec, wa_bd, wx_bd, w_out, *, tl=128):
    d = x.shape[-1]
    t = x.size // d
    width = w_out.shape[0]
    rows = SUBLANES * tl
    hist = SUBLANES * (CONV_WIDTH - 1)
    if x.ndim == 3:
        x_spec = pl.BlockSpec((SUBLANES, tl, d), lambda i: (0, i, 0))
        slabs = [pltpu.VMEM((d // LANES, rows, LANES), F32)]
    else:
        x_spec = pl.BlockSpec((rows, d), lambda i: (i, 0))
        slabs = []
    return pl.pallas_call(
        functools.partial(_rg_kernel, tl=tl),
        grid=(t // rows,),
        in_specs=[
            x_spec,
            _const_spec((None, 4, d), (layer, 0, 0)),
            _const_spec(w_in.shape, (0, 0)),
            _const_spec(pvec.shape, (0, 0)),
            _const_spec(wa_bd.shape, (0, 0, 0)),
            _const_spec(wx_bd.shape, (0, 0, 0)),
            _const_spec(w_out.shape, (0, 0)),
        ],
        out_specs=pl.BlockSpec((rows, d), lambda i: (i, 0)),
        out_shape=jax.ShapeDtypeStruct((t, d), F32),
        scratch_shapes=[
            pltpu.VMEM((rows + hist, width), F32),
            pltpu.VMEM((rows, width), F32),
            pltpu.VMEM((rows, width), F32),
            pltpu.VMEM((rows, width), F32),
            pltpu.VMEM((SUBLANES, width), F32),
        ] + slabs,
        compiler_params=pltpu.CompilerParams(
            dimension_semantics=("arbitrary",), vmem_limit_bytes=VMEM_LIMIT),
        name=f"rglru_{layer}",
    )(x, norm_g, w_in, pvec, wa_bd, wx_bd, w_out)


def _s5_prep_kernel(lre_ref, lim_ref, lstep_ref, bre_ref, bim_ref, ab_ref, bbre_ref, bbim_ref):
    lr = jnp.minimum(lre_ref[...], S5_EIG_CLIP)
    li = lim_ref[...]
    step = jnp.exp(lstep_ref[...])
    mag = jnp.exp(lr * step)
    ab_re = mag * jnp.cos(li * step)
    ab_im = mag * jnp.sin(li * step)
    den = lr * lr + li * li
    nr = ab_re - 1.0
    q_re = (nr * lr + ab_im * li) / den
    q_im = (ab_im * lr - nr * li) / den
    b_re = bre_ref[...]
    b_im = bim_ref[...]
    bbre_ref[...] = q_re * b_re - q_im * b_im
    bbim_ref[...] = q_re * b_im + q_im * b_re
    ab_ref[0:1, :] = ab_re
    ab_ref[1:2, :] = ab_im


def _s5_prep_call(lam_re, lam_im, log_step, b_re, b_im):
    n = lam_re.shape[1]
    c = b_re.shape[0]
    return pl.pallas_call(
        _s5_prep_kernel,
        out_shape=(jax.ShapeDtypeStruct((2, n), F32),
                   jax.ShapeDtypeStruct((c, n), F32),
                   jax.ShapeDtypeStruct((c, n), F32)),
        name="s5_discretize",
    )(lam_re, lam_im, log_step, b_re, b_im)


def _s5_kernel(x_ref, g_ref, win_ref, ab_ref, bre_ref, bim_ref, cre_ref, cim_ref, d_ref,
               wout_ref, o_ref, u_s, hre_s, him_s, y_s, cre_c, cim_c, *, tl):
    rows = SUBLANES * tl
    n_slabs, slab_ch, slab_st = bre_ref.shape
    half = slab_st // 2

    @pl.when(pl.program_id(0) == 0)
    def _():
        cre_c[...] = jnp.zeros(cre_c.shape, F32)
        cim_c[...] = jnp.zeros(cim_c.shape, F32)

    x = x_ref[...]
    g = g_ref[...]
    xn = _rms(x, g[0:1]).astype(BF16)
    u = _dot(xn, win_ref[...])
    u_s[...] = u
    ub = u.astype(BF16)

    for s in range(n_slabs):
        us = ub[:, s * slab_ch:(s + 1) * slab_ch]
        hre_s[...] = _dot(us, bre_ref[s])
        him_s[...] = _dot(us, bim_ref[s])
        for hf in range(2):
            lanes = slice(hf * half, (hf + 1) * half)
            st0 = s * slab_st + hf * half
            ar = jnp.broadcast_to(ab_ref[0:1, st0:st0 + half], (SUBLANES, half))
            ai = jnp.broadcast_to(ab_ref[1:2, st0:st0 + half], (SUBLANES, half))

            def step(t, carry, lanes=lanes, ar=ar, ai=ai):
                hr, hi = carry
                r0 = pl.multiple_of(t * SUBLANES, SUBLANES)
                nhr = ar * hr - ai * hi + hre_s[pl.ds(r0, SUBLANES), lanes]
                nhi = ar * hi + ai * hr + him_s[pl.ds(r0, SUBLANES), lanes]
                hre_s[pl.ds(r0, SUBLANES), lanes] = nhr
                him_s[pl.ds(r0, SUBLANES), lanes] = nhi
                return nhr, nhi

            hr, hi = lax.fori_loop(0, tl, step, (cre_c[s, :, lanes], cim_c[s, :, lanes]),
                                   unroll=True)
            cre_c[s, :, lanes] = hr
            cim_c[s, :, lanes] = hi
        y_s[:, s * slab_ch:(s + 1) * slab_ch] = (
            _dot(hre_s[...].astype(BF16), cre_ref[s]) - _dot(him_s[...].astype(BF16), cim_ref[s]))

    y = y_s[...] + d_ref[...] * u_s[...]
    y = jax.nn.gelu(y, approximate=True).astype(BF16)
    vo = _dot(y, wout_ref[...])
    d_out = vo.shape[1] // 2
    glu = vo[:, :d_out] * _sigmoid(vo[:, d_out:])
    o_ref[...] = x + _rms(glu, g[1:2])


def _s5_call(xf, norm_g, layer, w_in, ab, bre_bd, bim_bd, cre_bd, cim_bd, d_skip, w_out, *, tl=128):
    t, d = xf.shape
    width = w_in.shape[1]
    rows = SUBLANES * tl
    n_slabs, _, slab_st = bre_bd.shape
    return pl.pallas_call(
        functools.partial(_s5_kernel, tl=tl),
        grid=(t // rows,),
        in_specs=[
            pl.BlockSpec((rows, d), lambda i: (i, 0)),
            _const_spec((None, 4, d), (layer, 0, 0)),
            _const_spec(w_in.shape, (0, 0)),
            _const_spec(ab.shape, (0, 0)),
            _const_spec(bre_bd.shape, (0, 0, 0)),
            _const_spec(bim_bd.shape, (0, 0, 0)),
            _const_spec(cre_bd.shape, (0, 0, 0)),
            _const_spec(cim_bd.shape, (0, 0, 0)),
            _const_spec(d_skip.shape, (0, 0)),
            _const_spec(w_out.shape, (0, 0)),
        ],
        out_specs=pl.BlockSpec((rows, d), lambda i: (i, 0)),
        out_shape=jax.ShapeDtypeStruct((t, d), F32),
        scratch_shapes=[
            pltpu.VMEM((rows, width), F32),
            pltpu.VMEM((rows, slab_st), F32),
            pltpu.VMEM((rows, slab_st), F32),
            pltpu.VMEM((rows, width), F32),
            pltpu.VMEM((n_slabs, SUBLANES, slab_st), F32),
            pltpu.VMEM((n_slabs, SUBLANES, slab_st), F32),
        ],
        compiler_params=pltpu.CompilerParams(
            dimension_semantics=("arbitrary",), vmem_limit_bytes=VMEM_LIMIT),
        name=f"s5_{layer}",
    )(xf, norm_g, w_in, ab, bre_bd, bim_bd, cre_bd, cim_bd, d_skip, w_out)


def _cumsum_time(v):
    n = v.shape[0]
    d = SUBLANES
    while d < n:
        v = v + jnp.concatenate([jnp.zeros((d, v.shape[1]), v.dtype), v[:n - d]], axis=0)
        d *= 2
    return v


def _ssd_kernel(x_ref, g_ref, wg_ref, wdt_ref, cw_ref, hp_ref, cp_ref, wout_ref, o_ref,
                xn_s, ext, hist, sl, y_sl, csb_s, cst_s, wbt_s, st):
    rows = x_ref.shape[0]
    n_groups = wg_ref.shape[0]
    gw = wout_ref.shape[0] // n_groups
    hpg = gw // SSD_HEAD_DIM
    conv_w = gw + 2 * SSD_STATE
    x_sl = gw // 128
    hist_rows = SUBLANES * (CONV_WIDTH - 1)
    last = SSD_CHUNK - 1

    @pl.when(pl.program_id(0) == 0)
    def _():
        hist[...] = jnp.zeros(hist.shape, F32)
        st[...] = jnp.zeros(st.shape, F32)

    def batch_rows(b):
        return pl.ds(b, SSD_CHUNK, stride=SUBLANES)

    g = g_ref[...]
    hp = hp_ref[...]
    xn_s[...] = _rms(x_ref[...], g[0:1]).astype(BF16)

    dt = _softplus(_dot(xn_s[...], wdt_ref[...]) + hp[0:1])
    cs = _cumsum_time(dt * (-jnp.exp(hp[1:2])))
    cs_end = jnp.broadcast_to(cs[rows - SUBLANES:rows][None], (SSD_CHUNK, SUBLANES, cs.shape[1]))
    cs2 = cs * LOG2_E
    sl[0] = cs2
    sl[1] = cs2 - jnp.log2(dt)
    sl[2] = dt * jnp.exp(cs_end.reshape(cs.shape) - cs)
    for b in range(SUBLANES):
        csb_s[b] = sl[0, batch_rows(b), :]
        cst_s[b] = sl[1, batch_rows(b), :].T
        wbt_s[b] = sl[2, batch_rows(b), :].T

    ii = lax.broadcasted_iota(jnp.int32, (SSD_CHUNK, SSD_CHUNK), 0)
    jj = lax.broadcasted_iota(jnp.int32, (SSD_CHUNK, SSD_CHUNK), 1)
    tri = ii >= jj
    first_head = jj < SSD_HEAD_DIM
    col_head = lax.broadcasted_iota(jnp.int32, (SSD_CHUNK, gw), 1) // SSD_HEAD_DIM

    def conv_to_slabs(gi, pr):
        cw = cw_ref[gi]
        ext[0:hist_rows, :] = hist[gi]
        ext[hist_rows:hist_rows + rows, :] = pr
        acc = cw[CONV_WIDTH:CONV_WIDTH + 1]
        for k in range(CONV_WIDTH):
            acc = acc + cw[k:k + 1] * ext[SUBLANES * k:SUBLANES * k + rows, :]
        hist[gi] = ext[rows:rows + hist_rows, :]
        xbc = acc * _sigmoid(acc)
        for j in range(conv_w // 128):
            sl[j] = xbc[:, j * 128:(j + 1) * 128]

    def gate_and_project_out(gi, z):
        y = jnp.concatenate([y_sl[j] for j in range(x_sl)], axis=1)
        v = y * (z * _sigmoid(z))
        vn = v * lax.rsqrt(jnp.mean(v * v, axis=-1, keepdims=True) + RMS_EPS)
        vn = (vn * cp_ref[1:2, gi * gw:(gi + 1) * gw]).astype(BF16)
        part = _dot(vn, wout_ref[gi * gw:(gi + 1) * gw, :])
        if gi == 0:
            o_ref[...] = part
        else:
            o_ref[...] += part

    for gi in range(n_groups):
        pr = _dot(xn_s[...], wg_ref[gi])
        conv_to_slabs(gi, pr[:, :conv_w])
        z = pr[:, conv_w:conv_w + gw]
        d_skip = cp_ref[0:1, gi * gw:(gi + 1) * gw]

        def per_batch_block(blk, carry, gi=gi, d_skip=d_skip):
            bs = [blk * SSD_BATCH_BLOCK + k for k in range(SSD_BATCH_BLOCK)]
            xg = [jnp.concatenate([sl[j, batch_rows(b), :] for j in range(x_sl)], axis=1)
                  for b in bs]
            bg = [sl[x_sl, batch_rows(b), :] for b in bs]
            cg = [sl[x_sl + 1, batch_rows(b), :].astype(BF16) for b in bs]
            state = [st[b, gi] for b in bs]
            cb = [lax.dot_general(c, bb.astype(BF16), (((1,), (1,)), ((), ())),
                                  preferred_element_type=F32) for c, bb in zip(cg, bg)]
            y_off = [_dot(c, s.astype(BF16)) for c, s in zip(cg, state)]
            cb = [v.astype(BF16) for v in cb]
            bg_t = [bb.T.astype(BF16) for bb in bg]
            lhs, eg = [], []
            for i, b in enumerate(bs):
                cs_b, cst_b, wbt_b = csb_s[b], cst_s[b], wbt_s[b].astype(BF16)
                ms, ws, es = [], [], []
                for r in range(hpg):
                    h = gi * hpg + r
                    cs_col = jnp.broadcast_to(cs_b[:, h:h + 1], (SSD_CHUNK, SSD_CHUNK))
                    lm = jnp.exp2(jnp.where(tri, cs_col - cst_b[h:h + 1, :], -jnp.inf))
                    ms.append(cb[i] * lm.astype(BF16))
                    ws.append(bg_t[i] * wbt_b[h:h + 1, :])
                    es.append(jnp.exp2(cs_col))
                lhs.append(jnp.concatenate(
                    [jnp.concatenate(ms, axis=1), jnp.concatenate(ws, axis=1)], axis=0))
                eg.append(jnp.concatenate(
                    [jnp.where(first_head, es[2 * q], es[2 * q + 1]) for q in range(hpg // 2)],
                    axis=1))
            res = []
            for i in range(SSD_BATCH_BLOCK):
                xgb = xg[i].astype(BF16)
                bd = jnp.concatenate(
                    [jnp.where(col_head == r, xgb, jnp.zeros_like(xgb)) for r in range(hpg)],
                    axis=0)
                res.append(_dot(lhs[i], bd))
            for i, b in enumerate(bs):
                y = res[i][:SSD_CHUNK] + y_off[i] * eg[i] + xg[i] * d_skip
                for j in range(x_sl):
                    y_sl[j, batch_rows(b), :] = y[:, j * 128:(j + 1) * 128]
                st[b, gi] = eg[i][last:last + 1, :] * state[i] + res[i][SSD_CHUNK:]
            return carry

        lax.fori_loop(0, SUBLANES // SSD_BATCH_BLOCK, per_batch_block, 0)
        gate_and_project_out(gi, z)

    o_ref[...] = x_ref[...] + _rms(o_ref[...], g[1:2])


def _ssd_call(xf, norm_g, layer, wg, wdt, cw, hp, cp, w_out):
    t, d = xf.shape
    rows = SUBLANES * SSD_CHUNK
    n_groups = wg.shape[0]
    gw = w_out.shape[0] // n_groups
    conv_w = gw + 2 * SSD_STATE
    hist_rows = SUBLANES * (CONV_WIDTH - 1)
    per_batch = (SUBLANES, SSD_CHUNK, 128)
    return pl.pallas_call(
        _ssd_kernel,
        grid=(t // rows,),
        in_specs=[
            pl.BlockSpec((rows, d), lambda i: (i, 0)),
            _const_spec((None, 4, d), (layer, 0, 0)),
            _const_spec(wg.shape, (0, 0, 0)),
            _const_spec(wdt.shape, (0, 0)),
            _const_spec(cw.shape, (0, 0, 0)),
            _const_spec(hp.shape, (0, 0)),
            _const_spec(cp.shape, (0, 0)),
            _const_spec(w_out.shape, (0, 0)),
        ],
        out_specs=pl.BlockSpec((rows, d), lambda i: (i, 0)),
        out_shape=jax.ShapeDtypeStruct((t, d), F32),
        scratch_shapes=[
            pltpu.VMEM((rows, d), BF16),
            pltpu.VMEM((rows + hist_rows, conv_w), F32),
            pltpu.VMEM((n_groups, hist_rows, conv_w), F32),
            pltpu.VMEM((conv_w // 128, rows, 128), F32),
            pltpu.VMEM((gw // 128, rows, 128), F32),
            pltpu.VMEM(per_batch, F32),
            pltpu.VMEM(per_batch, F32),
            pltpu.VMEM(per_batch, F32),
            pltpu.VMEM((SUBLANES, n_groups, SSD_STATE, gw), F32),
        ],
        compiler_params=pltpu.CompilerParams(
            dimension_semantics=("arbitrary",), vmem_limit_bytes=VMEM_LIMIT),
        name=f"ssd_{layer}",
    )(xf, norm_g, wg, wdt, cw, hp, cp, w_out)


def _block_diag(blocks):
    n, r, c = blocks.shape
    eye = jnp.eye(n, dtype=bool)[:, None, :, None]
    out = jnp.where(eye, blocks[:, :, None, :], jnp.zeros((), blocks.dtype))
    return out.reshape(n * r, n * c)


def _slab_block_diag(blocks, per_slab):
    n, r, c = blocks.shape
    slabs = blocks.reshape(n // per_slab, per_slab, r, c)
    return jax.vmap(_block_diag)(slabs)


def kernel(x, norm_g, mlp_w1, mlp_w2, rg_w_in, rg_conv_w, rg_conv_b, rg_w_a, rg_b_a, rg_w_x, rg_b_x, rg_lam, rg_w_out, ssd_w_in, ssd_conv_w, ssd_conv_b, ssd_dt_bias, ssd_a_log, ssd_d, ssd_norm_g, ssd_w_out, s5_w_in, s5_lam_re, s5_lam_im, s5_log_step, s5_b_re, s5_b_im, s5_c_re, s5_c_im, s5_d, s5_w_out):
    bsz, seq, d = x.shape
    assert bsz == SUBLANES, "time-major mixers put the batch on the 8 sublanes"
    depth = norm_g.shape[0]
    n_mixers = 3

    w1 = mlp_w1.astype(BF16)
    w2 = mlp_w2.astype(BF16)

    for layer in range(depth):
        kind = layer % n_mixers
        j = layer // n_mixers
        if kind == 1:
            inner = ssd_w_out.shape[1]
            conv_dim = ssd_conv_w.shape[2]
            heads = ssd_dt_bias.shape[1]
            gn = SSD_GROUPS * SSD_STATE

            def by_group(m):
                r = m.shape[0]
                parts = [m[:, :inner].reshape(r, SSD_GROUPS, -1),
                         m[:, inner:inner + gn].reshape(r, SSD_GROUPS, -1),
                         m[:, inner + gn:].reshape(r, SSD_GROUPS, -1)]
                return jnp.concatenate(parts, axis=2).transpose(1, 0, 2)

            w_in = ssd_w_in[j].astype(BF16)
            wz = w_in[:, :inner].reshape(d, SSD_GROUPS, -1).transpose(1, 0, 2)
            wg = jnp.concatenate([by_group(w_in[:, inner:inner + conv_dim]), wz], axis=2)
            wdt = jnp.pad(w_in[:, inner + conv_dim:], ((0, 0), (0, 128 - heads)))
            cw = by_group(jnp.concatenate(
                [ssd_conv_w[j], ssd_conv_b[j][None], jnp.zeros((3, conv_dim), F32)], axis=0))
            hp = jnp.zeros((SUBLANES, 128), F32)
            hp = hp.at[0, :heads].set(ssd_dt_bias[j]).at[1, :heads].set(ssd_a_log[j])
            cp = jnp.zeros((SUBLANES, inner), F32)
            cp = cp.at[0].set(jnp.repeat(ssd_d[j], SSD_HEAD_DIM)).at[1].set(ssd_norm_g[j])
            x = _ssd_call(x, norm_g, layer, wg, wdt, cw, hp, cp, ssd_w_out[j].astype(BF16))
        else:
            if kind == 0:
                heads = rg_w_a.shape[1]
                per = RG_BLOCK // rg_w_a.shape[2]
                pvec = jnp.concatenate(
                    [rg_conv_w[j], rg_conv_b[j][None], rg_b_a[j][None], rg_b_x[j][None],
                     rg_lam[j][None]], axis=0)
                x = _rg_call(x, norm_g, layer, rg_w_in[j].astype(BF16), pvec,
                             _slab_block_diag(rg_w_a[j].astype(BF16), per),
                             _slab_block_diag(rg_w_x[j].astype(BF16), per),
                             rg_w_out[j].astype(BF16))
            else:
                groups, states, gch = s5_b_re.shape[1:]
                n = groups * states
                per = S5_SLAB // gch
                ab, bb_re, bb_im = _s5_prep_call(
                    s5_lam_re[j].reshape(1, n), s5_lam_im[j].reshape(1, n),
                    jnp.repeat(s5_log_step[j], states).reshape(1, n),
                    s5_b_re[j].transpose(2, 0, 1).reshape(gch, n),
                    s5_b_im[j].transpose(2, 0, 1).reshape(gch, n))

                def b_slabs(bb):
                    blocks = bb.reshape(gch, groups, states).transpose(1, 0, 2)
                    return _slab_block_diag(blocks.astype(BF16), per)

                def c_slabs(cc):
                    return _slab_block_diag(cc.transpose(0, 2, 1).astype(BF16), per)

                x = _s5_call(x, norm_g, layer, s5_w_in[j].astype(BF16), ab,
                             b_slabs(bb_re), b_slabs(bb_im), c_slabs(s5_c_re[j]),
                             c_slabs(s5_c_im[j]), s5_d[j].reshape(1, -1),
                             s5_w_out[j].astype(BF16))
        x = _mlp_call(x, norm_g, w1, w2, layer, batch_major_out=(layer == depth - 1))

    return x
```

```python
import functools

import jax
import jax.numpy as jnp
from jax import lax
from jax.experimental import pallas as pl
from jax.experimental.pallas import tpu as pltpu

F32 = jnp.float32
BF16 = jnp.bfloat16

RMS_EPS = 1e-6
LOG2_E = 1.4426950408889634
GELU_C0 = 0.7978845608028654
GELU_C1 = 0.044715 * GELU_C0
CONV_WIDTH = 4
RG_C = 8.0
RG_BLOCK = 256
SSD_CHUNK = 128
SSD_HEAD_DIM = 64
SSD_GROUPS = 8
SSD_STATE = 128
SSD_BATCH_BLOCK = 4
S5_EIG_CLIP = -1e-4
S5_SLAB = 256

SUBLANES = 8
LANES = 128
VMEM_LIMIT = 60 * 1024 * 1024


def _rms(x, g):
    return x * lax.rsqrt(jnp.mean(x * x, axis=-1, keepdims=True) + RMS_EPS) * g


def _softplus(x):
    return jnp.maximum(x, 0.0) + jnp.log1p(jnp.exp(-jnp.abs(x)))


def _sigmoid(x):
    return 0.5 * jnp.tanh(0.5 * x) + 0.5


def _silu(x):
    h = 0.5 * x
    return h + h * jnp.tanh(h)


def _gelu_tanh(x):
    h = 0.5 * x
    inner = x * (GELU_C0 + GELU_C1 * (x * x))
    return h + h * jnp.tanh(inner)


def _dot(a, b):
    return jnp.dot(a, b, preferred_element_type=F32)


def _const_spec(shape, index):
    return pl.BlockSpec(shape, lambda *_: index, pipeline_mode=pl.Buffered(1))


def _batch_rows(b, n):
    return pl.ds(b, n, stride=SUBLANES)


def _to_time_major(x_ref, slabs):
    bsz, n, d = x_ref.shape
    for b in range(bsz):
        for j in range(d // LANES):
            slabs[j, _batch_rows(b, n), :] = x_ref[b, :, j * LANES:(j + 1) * LANES]
    return jnp.concatenate([slabs[j] for j in range(d // LANES)], axis=1)


def _store_batch_major(o_ref, v, slabs):
    bsz, n, d = o_ref.shape
    for j in range(d // LANES):
        slabs[j] = v[:, j * LANES:(j + 1) * LANES]
    for b in range(bsz):
        for j in range(d // LANES):
            o_ref[b, :, j * LANES:(j + 1) * LANES] = slabs[j, _batch_rows(b, n), :]


def _mlp_kernel(x_ref, g_ref, w1_ref, w2_ref, o_ref, *slabs, ff_chunk):
    x = x_ref[...]
    g = g_ref[...]
    xn = _rms(x, g[2:3]).astype(BF16)
    d_ff = w1_ref.shape[1]
    acc = jnp.zeros(x.shape, F32)
    for c in range(d_ff // ff_chunk):
        cols = slice(c * ff_chunk, (c + 1) * ff_chunk)
        h = _dot(xn, w1_ref[:, cols])
        h = jnp.square(jnp.maximum(h, 0.0)).astype(BF16)
        acc = acc + _dot(h, w2_ref[cols, :])
    out = x + _rms(acc, g[3:4])
    if slabs:
        _store_batch_major(o_ref, out, slabs[0])
    else:
        o_ref[...] = out


def _mlp_call(xf, norm_g, w1, w2, layer, *, batch_major_out=False, tm=1024, ff_chunk=1024):
    t, d = xf.shape
    d_ff = w1.shape[-1]
    if batch_major_out:
        steps = tm // SUBLANES
        out_spec = pl.BlockSpec((SUBLANES, steps, d), lambda i: (0, i, 0))
        out_shape = jax.ShapeDtypeStruct((SUBLANES, t // SUBLANES, d), F32)
        scratch = [pltpu.VMEM((d // LANES, tm, LANES), F32)]
    else:
        out_spec = pl.BlockSpec((tm, d), lambda i: (i, 0))
        out_shape = jax.ShapeDtypeStruct((t, d), F32)
        scratch = []
    return pl.pallas_call(
        functools.partial(_mlp_kernel, ff_chunk=ff_chunk),
        grid=(t // tm,),
        in_specs=[
            pl.BlockSpec((tm, d), lambda i: (i, 0)),
            _const_spec((None, 4, d), (layer, 0, 0)),
            _const_spec((None, d, d_ff), (layer, 0, 0)),
            _const_spec((None, d_ff, d), (layer, 0, 0)),
        ],
        out_specs=out_spec,
        out_shape=out_shape,
        scratch_shapes=scratch,
        compiler_params=pltpu.CompilerParams(
            dimension_semantics=("parallel",), vmem_limit_bytes=VMEM_LIMIT),
        name=f"mlp_{layer}",
    )(xf, norm_g, w1, w2)


def _rg_kernel(x_ref, g_ref, win_ref, p_ref, wa_ref, wx_ref, wout_ref, o_ref,
               uext, a_s, b_s, gl_s, hc, *slabs, tl):
    rows = SUBLANES * tl
    hist = SUBLANES * (CONV_WIDTH - 1)
    width = wout_ref.shape[0]

    @pl.when(pl.program_id(0) == 0)
    def _():
        uext[0:hist, :] = jnp.zeros((hist, width), F32)
        hc[...] = jnp.zeros(hc.shape, F32)

    x = _to_time_major(x_ref, slabs[0]) if slabs else x_ref[...]
    g = g_ref[...]
    p = p_ref[...]
    xn = _rms(x, g[0:1]).astype(BF16)
    gl_s[...] = _gelu_tanh(_dot(xn, win_ref[:, :width]))
    uext[hist:hist + rows, :] = _dot(xn, win_ref[:, width:])

    uc = p[4:5]
    for k in range(CONV_WIDTH):
        uc = uc + p[k:k + 1] * uext[SUBLANES * k:SUBLANES * k + rows, :]
    uext[0:hist, :] = uext[rows:rows + hist, :]

    ucb = uc.astype(BF16)
    ra, rx = [], []
    for s in range(width // RG_BLOCK):
        blk = ucb[:, s * RG_BLOCK:(s + 1) * RG_BLOCK]
        ra.append(_dot(blk, wa_ref[s]))
        rx.append(_dot(blk, wx_ref[s]))
    r = _sigmoid(jnp.concatenate(ra, axis=1) + p[5:6])
    ig = _sigmoid(jnp.concatenate(rx, axis=1) + p[6:7])
    a = jnp.exp2(r * ((-RG_C * LOG2_E) * _softplus(-p[7:8])))
    a_s[...] = a
    b_s[...] = jnp.sqrt(1.0 - a * a) * (ig * uc)

    def step(t, h):
        r0 = pl.multiple_of(t * SUBLANES, SUBLANES)
        h = a_s[pl.ds(r0, SUBLANES), :] * h + b_s[pl.ds(r0, SUBLANES), :]
        b_s[pl.ds(r0, SUBLANES), :] = h
        return h

    hc[...] = lax.fori_loop(0, tl, step, hc[...], unroll=True)

    y = (b_s[...] * gl_s[...]).astype(BF16)
    o_ref[...] = x + _rms(_dot(y, wout_ref[...]), g[1:2])


def _rg_call(x, norm_g, layer, w_in, pvec, wa_bd, wx_bd, w_out, *, tl=128):
    d = x.shape[-1]
    t = x.size // d
    width = w_out.shape[0]
    rows = SUBLANES * tl
    hist = SUBLANES * (CONV_WIDTH - 1)
    if x.ndim == 3:
        x_spec = pl.BlockSpec((SUBLANES, tl, d), lambda i: (0, i, 0))
        slabs = [pltpu.VMEM((d // LANES, rows, LANES), F32)]
    else:
        x_spec = pl.BlockSpec((rows, d), lambda i: (i, 0))
        slabs = []
    return pl.pallas_call(
        functools.partial(_rg_kernel, tl=tl),
        grid=(t // rows,),
        in_specs=[
            x_spec,
            _const_spec((None, 4, d), (layer, 0, 0)),
            _const_spec(w_in.shape, (0, 0)),
            _const_spec(pvec.shape, (0, 0)),
            _const_spec(wa_bd.shape, (0, 0, 0)),
            _const_spec(wx_bd.shape, (0, 0, 0)),
            _const_spec(w_out.shape, (0, 0)),
        ],
        out_specs=pl.BlockSpec((rows, d), lambda i: (i, 0)),
        out_shape=jax.ShapeDtypeStruct((t, d), F32),
        scratch_shapes=[
            pltpu.VMEM((rows + hist, width), F32),
            pltpu.VMEM((rows, width), F32),
            pltpu.VMEM((rows, width), F32),
            pltpu.VMEM((rows, width), F32),
            pltpu.VMEM((SUBLANES, width), F32),
        ] + slabs,
        compiler_params=pltpu.CompilerParams(
            dimension_semantics=("arbitrary",), vmem_limit_bytes=VMEM_LIMIT),
        name=f"rglru_{layer}",
    )(x, norm_g, w_in, pvec, wa_bd, wx_bd, w_out)


def _s5_prep_kernel(lre_ref, lim_ref, lstep_ref, bre_ref, bim_ref, ab_ref, bbre_ref, bbim_ref):
    lr = jnp.minimum(lre_ref[...], S5_EIG_CLIP)
    li = lim_ref[...]
    step = jnp.exp(lstep_ref[...])
    mag = jnp.exp(lr * step)
    ab_re = mag * jnp.cos(li * step)
    ab_im = mag * jnp.sin(li * step)
    den = lr * lr + li * li
    nr = ab_re - 1.0
    q_re = (nr * lr + ab_im * li) / den
    q_im = (ab_im * lr - nr * li) / den
    b_re = bre_ref[...]
    b_im = bim_ref[...]
    bbre_ref[...] = q_re * b_re - q_im * b_im
    bbim_ref[...] = q_re * b_im + q_im * b_re
    ab_ref[0:1, :] = ab_re
    ab_ref[1:2, :] = ab_im


def _s5_prep_call(lam_re, lam_im, log_step, b_re, b_im):
    n = lam_re.shape[1]
    c = b_re.shape[0]
    return pl.pallas_call(
        _s5_prep_kernel,
        out_shape=(jax.ShapeDtypeStruct((2, n), F32),
                   jax.ShapeDtypeStruct((c, n), F32),
                   jax.ShapeDtypeStruct((c, n), F32)),
        name="s5_discretize",
    )(lam_re, lam_im, log_step, b_re, b_im)


def _s5_kernel(x_ref, g_ref, win_ref, ab_ref, bre_ref, bim_ref, cre_ref, cim_ref, d_ref,
               wout_ref, o_ref, u_s, hre_s, him_s, y_s, cre_c, cim_c, *, tl):
    rows = SUBLANES * tl
    n_slabs, slab_ch, slab_st = bre_ref.shape
    half = slab_st // 2

    @pl.when(pl.program_id(0) == 0)
    def _():
        cre_c[...] = jnp.zeros(cre_c.shape, F32)
        cim_c[...] = jnp.zeros(cim_c.shape, F32)

    x = x_ref[...]
    g = g_ref[...]
    xn = _rms(x, g[0:1]).astype(BF16)
    u = _dot(xn, win_ref[...])
    u_s[...] = u
    ub = u.astype(BF16)

    for s in range(n_slabs):
        us = ub[:, s * slab_ch:(s + 1) * slab_ch]
        hre_s[...] = _dot(us, bre_ref[s])
        him_s[...] = _dot(us, bim_ref[s])
        for hf in range(2):
            lanes = slice(hf * half, (hf + 1) * half)
            st0 = s * slab_st + hf * half
            ar = jnp.broadcast_to(ab_ref[0:1, st0:st0 + half], (SUBLANES, half))
            ai = jnp.broadcast_to(ab_ref[1:2, st0:st0 + half], (SUBLANES, half))

            def step(t, carry, lanes=lanes, ar=ar, ai=ai):
                hr, hi = carry
                r0 = pl.multiple_of(t * SUBLANES, SUBLANES)
                nhr = ar * hr - ai * hi + hre_s[pl.ds(r0, SUBLANES), lanes]
                nhi = ar * hi + ai * hr + him_s[pl.ds(r0, SUBLANES), lanes]
                hre_s[pl.ds(r0, SUBLANES), lanes] = nhr
                him_s[pl.ds(r0, SUBLANES), lanes] = nhi
                return nhr, nhi

            hr, hi = lax.fori_loop(0, tl, step, (cre_c[s, :, lanes], cim_c[s, :, lanes]),
                                   unroll=True)
            cre_c[s, :, lanes] = hr
            cim_c[s, :, lanes] = hi
        y_s[:, s * slab_ch:(s + 1) * slab_ch] = (
            _dot(hre_s[...].astype(BF16), cre_ref[s]) - _dot(him_s[...].astype(BF16), cim_ref[s]))

    y = y_s[...] + d_ref[...] * u_s[...]
    y = _gelu_tanh(y).astype(BF16)
    vo = _dot(y, wout_ref[...])
    d_out = vo.shape[1] // 2
    glu = vo[:, :d_out] * _sigmoid(vo[:, d_out:])
    o_ref[...] = x + _rms(glu, g[1:2])


def _s5_call(xf, norm_g, layer, w_in, ab, bre_bd, bim_bd, cre_bd, cim_bd, d_skip, w_out, *, tl=128):
    t, d = xf.shape
    width = w_in.shape[1]
    rows = SUBLANES * tl
    n_slabs, _, slab_st = bre_bd.shape
    return pl.pallas_call(
        functools.partial(_s5_kernel, tl=tl),
        grid=(t // rows,),
        in_specs=[
            pl.BlockSpec((rows, d), lambda i: (i, 0)),
            _const_spec((None, 4, d), (layer, 0, 0)),
            _const_spec(w_in.shape, (0, 0)),
            _const_spec(ab.shape, (0, 0)),
            _const_spec(bre_bd.shape, (0, 0, 0)),
            _const_spec(bim_bd.shape, (0, 0, 0)),
            _const_spec(cre_bd.shape, (0, 0, 0)),
            _const_spec(cim_bd.shape, (0, 0, 0)),
            _const_spec(d_skip.shape, (0, 0)),
            _const_spec(w_out.shape, (0, 0)),
        ],
        out_specs=pl.BlockSpec((rows, d), lambda i: (i, 0)),
        out_shape=jax.ShapeDtypeStruct((t, d), F32),
        scratch_shapes=[
            pltpu.VMEM((rows, width), F32),
            pltpu.VMEM((rows, slab_st), F32),
            pltpu.VMEM((rows, slab_st), F32),
            pltpu.VMEM((rows, width), F32),
            pltpu.VMEM((n_slabs, SUBLANES, slab_st), F32),
            pltpu.VMEM((n_slabs, SUBLANES, slab_st), F32),
        ],
        compiler_params=pltpu.CompilerParams(
            dimension_semantics=("arbitrary",), vmem_limit_bytes=VMEM_LIMIT),
        name=f"s5_{layer}",
    )(xf, norm_g, w_in, ab, bre_bd, bim_bd, cre_bd, cim_bd, d_skip, w_out)


def _cumsum_time(v):
    n = v.shape[0]
    d = SUBLANES
    while d < n:
        v = v + jnp.concatenate([jnp.zeros((d, v.shape[1]), v.dtype), v[:n - d]], axis=0)
        d *= 2
    return v


def _ssd_kernel(x_ref, g_ref, wg_ref, wdt_ref, cw_ref, hp_ref, cp_ref, wout_ref, o_ref,
                xn_s, vn_s, ext, hist, sl, y_sl, csb_s, cst_s, wbt_s, st):
    rows = x_ref.shape[0]
    n_groups = wg_ref.shape[0]
    gw = wout_ref.shape[0] // n_groups
    hpg = gw // SSD_HEAD_DIM
    conv_w = gw + 2 * SSD_STATE
    x_sl = gw // 128
    hist_rows = SUBLANES * (CONV_WIDTH - 1)
    last = SSD_CHUNK - 1

    @pl.when(pl.program_id(0) == 0)
    def _():
        hist[...] = jnp.zeros(hist.shape, F32)
        st[...] = jnp.zeros(st.shape, F32)

    def batch_rows(b):
        return pl.ds(b, SSD_CHUNK, stride=SUBLANES)

    g = g_ref[...]
    hp = hp_ref[...]
    xn_s[...] = _rms(x_ref[...], g[0:1]).astype(BF16)

    dt = _softplus(_dot(xn_s[...], wdt_ref[...]) + hp[0:1])
    cs = _cumsum_time(dt * (-jnp.exp(hp[1:2])))
    cs_end = jnp.broadcast_to(cs[rows - SUBLANES:rows][None], (SSD_CHUNK, SUBLANES, cs.shape[1]))
    cs2 = cs * LOG2_E
    sl[0] = cs2
    sl[1] = cs2 - jnp.log2(dt)
    sl[2] = dt * jnp.exp(cs_end.reshape(cs.shape) - cs)
    for b in range(SUBLANES):
        csb_s[b] = sl[0, batch_rows(b), :]
        cst_s[b] = sl[1, batch_rows(b), :].T
        wbt_s[b] = sl[2, batch_rows(b), :].T

    ii = lax.broadcasted_iota(jnp.int32, (SSD_CHUNK, SSD_CHUNK), 0)
    jj = lax.broadcasted_iota(jnp.int32, (SSD_CHUNK, SSD_CHUNK), 1)
    tri = ii >= jj
    first_head = jj < SSD_HEAD_DIM
    col_head = lax.broadcasted_iota(jnp.int32, (SSD_CHUNK, gw), 1) // SSD_HEAD_DIM

    def conv_to_slabs(gi, pr):
        cw = cw_ref[gi]
        ext[0:hist_rows, :] = hist[gi]
        ext[hist_rows:hist_rows + rows, :] = pr
        acc = cw[CONV_WIDTH:CONV_WIDTH + 1]
        for k in range(CONV_WIDTH):
            acc = acc + cw[k:k + 1] * ext[SUBLANES * k:SUBLANES * k + rows, :]
        hist[gi] = ext[rows:rows + hist_rows, :]
        xbc = _silu(acc)
        for j in range(conv_w // 128):
            sl[j] = xbc[:, j * 128:(j + 1) * 128]

    def gate_and_project_out(gi, z):
        y = jnp.concatenate([y_sl[j] for j in range(x_sl)], axis=1)
        v = y * _silu(z)
        vn = v * lax.rsqrt(jnp.mean(v * v, axis=-1, keepdims=True) + RMS_EPS)
        half = gi % 2
        vn_s[:, half * gw:(half + 1) * gw] = (
            vn * cp_ref[1:2, gi * gw:(gi + 1) * gw]).astype(BF16)
        if half == 1:
            part = _dot(vn_s[...], wout_ref[(gi - 1) * gw:(gi + 1) * gw, :])
            if gi == 1:
                o_ref[...] = part
            else:
                o_ref[...] += part

    for gi in range(n_groups):
        pr = _dot(xn_s[...], wg_ref[gi])
        conv_to_slabs(gi, pr[:, :conv_w])
        z = pr[:, conv_w:conv_w + gw]
        d_skip = cp_ref[0:1, gi * gw:(gi + 1) * gw]

        def per_batch_block(blk, carry, gi=gi, d_skip=d_skip):
            bs = [blk * SSD_BATCH_BLOCK + k for k in range(SSD_BATCH_BLOCK)]
            xg = [jnp.concatenate([sl[j, batch_rows(b), :] for j in range(x_sl)], axis=1)
                  for b in bs]
            bg = [sl[x_sl, batch_rows(b), :] for b in bs]
            cg = [sl[x_sl + 1, batch_rows(b), :].astype(BF16) for b in bs]
            state = [st[b, gi] for b in bs]
            cb = [lax.dot_general(c, bb.astype(BF16), (((1,), (1,)), ((), ())),
                                  preferred_element_type=F32) for c, bb in zip(cg, bg)]
            y_off = [_dot(c, s.astype(BF16)) for c, s in zip(cg, state)]
            cb = [v.astype(BF16) for v in cb]
            bg_t = [bb.T.astype(BF16) for bb in bg]
            lhs, eg = [], []
            for i, b in enumerate(bs):
                cs_b, cst_b, wbt_b = csb_s[b], cst_s[b], wbt_s[b].astype(BF16)
                ms, ws, es = [], [], []
                for r in range(hpg):
                    h = gi * hpg + r
                    cs_col = jnp.broadcast_to(cs_b[:, h:h + 1], (SSD_CHUNK, SSD_CHUNK))
                    lm = jnp.exp2(jnp.where(tri, cs_col - cst_b[h:h + 1, :], -jnp.inf))
                    ms.append(cb[i] * lm.astype(BF16))
                    ws.append(bg_t[i] * wbt_b[h:h + 1, :])
                    es.append(jnp.exp2(cs_col))
                lhs.append(jnp.concatenate(
                    [jnp.concatenate(ms, axis=1), jnp.concatenate(ws, axis=1)], axis=0))
                eg.append(jnp.concatenate(
                    [jnp.where(first_head, es[2 * q], es[2 * q + 1]) for q in range(hpg // 2)],
                    axis=1))
            res = []
            for i in range(SSD_BATCH_BLOCK):
                xgb = xg[i].astype(BF16)
                bd = jnp.concatenate(
                    [jnp.where(col_head == r, xgb, jnp.zeros_like(xgb)) for r in range(hpg)],
                    axis=0)
                res.append(_dot(lhs[i], bd))
            for i, b in enumerate(bs):
                y = res[i][:SSD_CHUNK] + y_off[i] * eg[i] + xg[i] * d_skip
                for j in range(x_sl):
                    y_sl[j, batch_rows(b), :] = y[:, j * 128:(j + 1) * 128]
                st[b, gi] = eg[i][last:last + 1, :] * state[i] + res[i][SSD_CHUNK:]
            return carry

        lax.fori_loop(0, SUBLANES // SSD_BATCH_BLOCK, per_batch_block, 0)
        gate_and_project_out(gi, z)

    o_ref[...] = x_ref[...] + _rms(o_ref[...], g[1:2])


def _ssd_call(xf, norm_g, layer, wg, wdt, cw, hp, cp, w_out):
    t, d = xf.shape
    rows = SUBLANES * SSD_CHUNK
    n_groups = wg.shape[0]
    gw = w_out.shape[0] // n_groups
    conv_w = gw + 2 * SSD_STATE
    hist_rows = SUBLANES * (CONV_WIDTH - 1)
    per_batch = (SUBLANES, SSD_CHUNK, 128)
    return pl.pallas_call(
        _ssd_kernel,
        grid=(t // rows,),
        in_specs=[
            pl.BlockSpec((rows, d), lambda i: (i, 0)),
            _const_spec((None, 4, d), (layer, 0, 0)),
            _const_spec(wg.shape, (0, 0, 0)),
            _const_spec(wdt.shape, (0, 0)),
            _const_spec(cw.shape, (0, 0, 0)),
            _const_spec(hp.shape, (0, 0)),
            _const_spec(cp.shape, (0, 0)),
            _const_spec(w_out.shape, (0, 0)),
        ],
        out_specs=pl.BlockSpec((rows, d), lambda i: (i, 0)),
        out_shape=jax.ShapeDtypeStruct((t, d), F32),
        scratch_shapes=[
            pltpu.VMEM((rows, d), BF16),
            pltpu.VMEM((rows, 2 * gw), BF16),
            pltpu.VMEM((rows + hist_rows, conv_w), F32),
            pltpu.VMEM((n_groups, hist_rows, conv_w), F32),
            pltpu.VMEM((conv_w // 128, rows, 128), F32),
            pltpu.VMEM((gw // 128, rows, 128), F32),
            pltpu.VMEM(per_batch, F32),
            pltpu.VMEM(per_batch, F32),
            pltpu.VMEM(per_batch, F32),
            pltpu.VMEM((SUBLANES, n_groups, SSD_STATE, gw), F32),
        ],
        compiler_params=pltpu.CompilerParams(
            dimension_semantics=("arbitrary",), vmem_limit_bytes=VMEM_LIMIT),
        name=f"ssd_{layer}",
    )(xf, norm_g, wg, wdt, cw, hp, cp, w_out)


def _block_diag(blocks):
    n, r, c = blocks.shape
    eye = jnp.eye(n, dtype=bool)[:, None, :, None]
    out = jnp.where(eye, blocks[:, :, None, :], jnp.zeros((), blocks.dtype))
    return out.reshape(n * r, n * c)


def _slab_block_diag(blocks, per_slab):
    n, r, c = blocks.shape
    slabs = blocks.reshape(n // per_slab, per_slab, r, c)
    return jax.vmap(_block_diag)(slabs)


def kernel(x, norm_g, mlp_w1, mlp_w2, rg_w_in, rg_conv_w, rg_conv_b, rg_w_a, rg_b_a, rg_w_x, rg_b_x, rg_lam, rg_w_out, ssd_w_in, ssd_conv_w, ssd_conv_b, ssd_dt_bias, ssd_a_log, ssd_d, ssd_norm_g, ssd_w_out, s5_w_in, s5_lam_re, s5_lam_im, s5_log_step, s5_b_re, s5_b_im, s5_c_re, s5_c_im, s5_d, s5_w_out):
    bsz, seq, d = x.shape
    assert bsz == SUBLANES, "time-major mixers put the batch on the 8 sublanes"
    depth = norm_g.shape[0]
    n_mixers = 3

    w1 = mlp_w1.astype(BF16)
    w2 = mlp_w2.astype(BF16)

    for layer in range(depth):
        kind = layer % n_mixers
        j = layer // n_mixers
        if kind == 1:
            inner = ssd_w_out.shape[1]
            conv_dim = ssd_conv_w.shape[2]
            heads = ssd_dt_bias.shape[1]
            gn = SSD_GROUPS * SSD_STATE

            def by_group(m):
                r = m.shape[0]
                parts = [m[:, :inner].reshape(r, SSD_GROUPS, -1),
                         m[:, inner:inner + gn].reshape(r, SSD_GROUPS, -1),
                         m[:, inner + gn:].reshape(r, SSD_GROUPS, -1)]
                return jnp.concatenate(parts, axis=2).transpose(1, 0, 2)

            w_in = ssd_w_in[j].astype(BF16)
            wz = w_in[:, :inner].reshape(d, SSD_GROUPS, -1).transpose(1, 0, 2)
            wg = jnp.concatenate([by_group(w_in[:, inner:inner + conv_dim]), wz], axis=2)
            wdt = jnp.pad(w_in[:, inner + conv_dim:], ((0, 0), (0, 128 - heads)))
            cw = by_group(jnp.concatenate(
                [ssd_conv_w[j], ssd_conv_b[j][None], jnp.zeros((3, conv_dim), F32)], axis=0))
            hp = jnp.zeros((SUBLANES, 128), F32)
            hp = hp.at[0, :heads].set(ssd_dt_bias[j]).at[1, :heads].set(ssd_a_log[j])
            cp = jnp.zeros((SUBLANES, inner), F32)
            cp = cp.at[0].set(jnp.repeat(ssd_d[j], SSD_HEAD_DIM)).at[1].set(ssd_norm_g[j])
            x = _ssd_call(x, norm_g, layer, wg, wdt, cw, hp, cp, ssd_w_out[j].astype(BF16))
        else:
            if kind == 0:
                heads = rg_w_a.shape[1]
                per = RG_BLOCK // rg_w_a.shape[2]
                pvec = jnp.concatenate(
                    [rg_conv_w[j], rg_conv_b[j][None], rg_b_a[j][None], rg_b_x[j][None],
                     rg_lam[j][None]], axis=0)
                x = _rg_call(x, norm_g, layer, rg_w_in[j].astype(BF16), pvec,
                             _slab_block_diag(rg_w_a[j].astype(BF16), per),
                             _slab_block_diag(rg_w_x[j].astype(BF16), per),
                             rg_w_out[j].astype(BF16))
            else:
                groups, states, gch = s5_b_re.shape[1:]
                n = groups * states
                per = S5_SLAB // gch
                ab, bb_re, bb_im = _s5_prep_call(
                    s5_lam_re[j].reshape(1, n), s5_lam_im[j].reshape(1, n),
                    jnp.repeat(s5_log_step[j], states).reshape(1, n),
                    s5_b_re[j].transpose(2, 0, 1).reshape(gch, n),
                    s5_b_im[j].transpose(2, 0, 1).reshape(gch, n))

                def b_slabs(bb):
                    blocks = bb.reshape(gch, groups, states).transpose(1, 0, 2)
                    return _slab_block_diag(blocks.astype(BF16), per)

                def c_slabs(cc):
                    return _slab_block_diag(cc.transpose(0, 2, 1).astype(BF16), per)

                x = _s5_call(x, norm_g, layer, s5_w_in[j].astype(BF16), ab,
                             b_slabs(bb_re), b_slabs(bb_im), c_slabs(s5_c_re[j]),
                             c_slabs(s5_c_im[j]), s5_d[j].reshape(1, -1),
                             s5_w_out[j].astype(BF16))
        x = _mlp_call(x, norm_g, w1, w2, layer, batch_major_out=(layer == depth - 1))

    return x
```

```python
import functools

import jax
import jax.numpy as jnp
from jax import lax
from jax.experimental import pallas as pl
from jax.experimental.pallas import tpu as pltpu

F32 = jnp.float32
BF16 = jnp.bfloat16

RMS_EPS = 1e-6
LOG2_E = 1.4426950408889634
GELU_C0 = 0.7978845608028654
GELU_C1 = 0.044715 * GELU_C0
CONV_WIDTH = 4
RG_C = 8.0
RG_BLOCK = 256
SSD_CHUNK = 128
SSD_HEAD_DIM = 64
SSD_GROUPS = 8
SSD_STATE = 128
SSD_BATCH_BLOCK = 4
SSD_OUT_GROUPS = 2
S5_EIG_CLIP = -1e-4
S5_SLAB = 256

SUBLANES = 8
LANES = 128
VMEM_LIMIT = 60 * 1024 * 1024


def _rms(x, g):
    return x * lax.rsqrt(jnp.mean(x * x, axis=-1, keepdims=True) + RMS_EPS) * g


def _softplus(x):
    return jnp.maximum(x, 0.0) + jnp.log1p(jnp.exp(-jnp.abs(x)))


def _sigmoid(x):
    return 0.5 * jnp.tanh(0.5 * x) + 0.5


def _silu_of_twice(h):
    return h + h * jnp.tanh(h)


def _gelu_tanh(x):
    h = 0.5 * x
    inner = x * (GELU_C0 + GELU_C1 * (x * x))
    return h + h * jnp.tanh(inner)


def _gelu_tanh_of_twice(h):
    inner = h * (2.0 * GELU_C0 + (8.0 * GELU_C1) * (h * h))
    return h + h * jnp.tanh(inner)


def _dot(a, b):
    return jnp.dot(a, b, preferred_element_type=F32)


def _const_spec(shape, index):
    return pl.BlockSpec(shape, lambda *_: index, pipeline_mode=pl.Buffered(1))


def _batch_rows(b, n):
    return pl.ds(b, n, stride=SUBLANES)


def _to_time_major(x_ref, slabs):
    bsz, n, d = x_ref.shape
    for b in range(bsz):
        for j in range(d // LANES):
            slabs[j, _batch_rows(b, n), :] = x_ref[b, :, j * LANES:(j + 1) * LANES]
    return jnp.concatenate([slabs[j] for j in range(d // LANES)], axis=1)


def _store_batch_major(o_ref, v, slabs):
    bsz, n, d = o_ref.shape
    for j in range(d // LANES):
        slabs[j] = v[:, j * LANES:(j + 1) * LANES]
    for b in range(bsz):
        for j in range(d // LANES):
            o_ref[b, :, j * LANES:(j + 1) * LANES] = slabs[j, _batch_rows(b, n), :]


def _mlp_kernel(x_ref, g_ref, w1_ref, w2_ref, o_ref, *slabs, ff_chunk):
    x = x_ref[...]
    g = g_ref[...]
    xn = _rms(x, g[2:3]).astype(BF16)
    d_ff = w1_ref.shape[1]
    acc = jnp.zeros(x.shape, F32)
    for c in range(d_ff // ff_chunk):
        cols = slice(c * ff_chunk, (c + 1) * ff_chunk)
        h = _dot(xn, w1_ref[:, cols])
        h = jnp.square(jnp.maximum(h, 0.0)).astype(BF16)
        acc = acc + _dot(h, w2_ref[cols, :])
    out = x + _rms(acc, g[3:4])
    if slabs:
        _store_batch_major(o_ref, out, slabs[0])
    else:
        o_ref[...] = out


def _mlp_call(xf, norm_g, w1, w2, layer, *, batch_major_out=False, tm=1024, ff_chunk=1024):
    t, d = xf.shape
    d_ff = w1.shape[-1]
    if batch_major_out:
        steps = tm // SUBLANES
        out_spec = pl.BlockSpec((SUBLANES, steps, d), lambda i: (0, i, 0))
        out_shape = jax.ShapeDtypeStruct((SUBLANES, t // SUBLANES, d), F32)
        scratch = [pltpu.VMEM((d // LANES, tm, LANES), F32)]
    else:
        out_spec = pl.BlockSpec((tm, d), lambda i: (i, 0))
        out_shape = jax.ShapeDtypeStruct((t, d), F32)
        scratch = []
    return pl.pallas_call(
        functools.partial(_mlp_kernel, ff_chunk=ff_chunk),
        grid=(t // tm,),
        in_specs=[
            pl.BlockSpec((tm, d), lambda i: (i, 0)),
            _const_spec((None, 4, d), (layer, 0, 0)),
            _const_spec((None, d, d_ff), (layer, 0, 0)),
            _const_spec((None, d_ff, d), (layer, 0, 0)),
        ],
        out_specs=out_spec,
        out_shape=out_shape,
        scratch_shapes=scratch,
        compiler_params=pltpu.CompilerParams(
            dimension_semantics=("parallel",), vmem_limit_bytes=VMEM_LIMIT),
        name=f"mlp_{layer}",
    )(xf, norm_g, w1, w2)


def _rg_kernel(x_ref, g_ref, win_ref, p_ref, wa_ref, wx_ref, wout_ref, o_ref,
               uext, a_s, b_s, gl_s, hc, *slabs, tl):
    rows = SUBLANES * tl
    hist = SUBLANES * (CONV_WIDTH - 1)
    width = wout_ref.shape[0]

    @pl.when(pl.program_id(0) == 0)
    def _():
        uext[0:hist, :] = jnp.zeros((hist, width), F32)
        hc[...] = jnp.zeros(hc.shape, F32)

    x = _to_time_major(x_ref, slabs[0]) if slabs else x_ref[...]
    g = g_ref[...]
    p = p_ref[...]
    xn = _rms(x, g[0:1]).astype(BF16)
    gl_s[...] = _gelu_tanh_of_twice(_dot(xn, win_ref[:, :width]))
    uext[hist:hist + rows, :] = _dot(xn, win_ref[:, width:])

    uc = p[4:5]
    for k in range(CONV_WIDTH):
        uc = uc + p[k:k + 1] * uext[SUBLANES * k:SUBLANES * k + rows, :]
    uext[0:hist, :] = uext[rows:rows + hist, :]

    ucb = uc.astype(BF16)
    ra, rx = [], []
    for s in range(width // RG_BLOCK):
        blk = ucb[:, s * RG_BLOCK:(s + 1) * RG_BLOCK]
        ra.append(_dot(blk, wa_ref[s]))
        rx.append(_dot(blk, wx_ref[s]))
    ta = jnp.tanh(jnp.concatenate(ra, axis=1) + p[5:6])
    tx = jnp.tanh(jnp.concatenate(rx, axis=1) + p[6:7])
    k = (-0.5 * RG_C * LOG2_E) * _softplus(-p[7:8])
    a = jnp.exp2(k + k * ta)
    a_s[...] = a
    hu = 0.5 * uc
    b_s[...] = jnp.sqrt(1.0 - a * a) * (hu + hu * tx)

    def step(t, h):
        r0 = pl.multiple_of(t * SUBLANES, SUBLANES)
        h = a_s[pl.ds(r0, SUBLANES), :] * h + b_s[pl.ds(r0, SUBLANES), :]
        b_s[pl.ds(r0, SUBLANES), :] = h
        return h

    hc[...] = lax.fori_loop(0, tl, step, hc[...], unroll=True)

    y = (b_s[...] * gl_s[...]).astype(BF16)
    o_ref[...] = x + _rms(_dot(y, wout_ref[...]), g[1:2])


def _rg_call(x, norm_g, layer, w_in, pvec, wa_bd, wx_bd, w_out, *, tl=128):
    d = x.shape[-1]
    t = x.size // d
    width = w_out.shape[0]
    rows = SUBLANES * tl
    hist = SUBLANES * (CONV_WIDTH - 1)
    if x.ndim == 3:
        x_spec = pl.BlockSpec((SUBLANES, tl, d), lambda i: (0, i, 0))
        slabs = [pltpu.VMEM((d // LANES, rows, LANES), F32)]
    else:
        x_spec = pl.BlockSpec((rows, d), lambda i: (i, 0))
        slabs = []
    return pl.pallas_call(
        functools.partial(_rg_kernel, tl=tl),
        grid=(t // rows,),
        in_specs=[
            x_spec,
            _const_spec((None, 4, d), (layer, 0, 0)),
            _const_spec(w_in.shape, (0, 0)),
            _const_spec(pvec.shape, (0, 0)),
            _const_spec(wa_bd.shape, (0, 0, 0)),
            _const_spec(wx_bd.shape, (0, 0, 0)),
            _const_spec(w_out.shape, (0, 0)),
        ],
        out_specs=pl.BlockSpec((rows, d), lambda i: (i, 0)),
        out_shape=jax.ShapeDtypeStruct((t, d), F32),
        scratch_shapes=[
            pltpu.VMEM((rows + hist, width), F32),
            pltpu.VMEM((rows, width), F32),
            pltpu.VMEM((rows, width), F32),
            pltpu.VMEM((rows, width), F32),
            pltpu.VMEM((SUBLANES, width), F32),
        ] + slabs,
        compiler_params=pltpu.CompilerParams(
            dimension_semantics=("arbitrary",), vmem_limit_bytes=VMEM_LIMIT),
        name=f"rglru_{layer}",
    )(x, norm_g, w_in, pvec, wa_bd, wx_bd, w_out)


def _s5_prep_kernel(lre_ref, lim_ref, lstep_ref, bre_ref, bim_ref, ab_ref, bbre_ref, bbim_ref):
    lr = jnp.minimum(lre_ref[...], S5_EIG_CLIP)
    li = lim_ref[...]
    step = jnp.exp(lstep_ref[...])
    mag = jnp.exp(lr * step)
    ab_re = mag * jnp.cos(li * step)
    ab_im = mag * jnp.sin(li * step)
    den = lr * lr + li * li
    nr = ab_re - 1.0
    q_re = (nr * lr + ab_im * li) / den
    q_im = (ab_im * lr - nr * li) / den
    b_re = bre_ref[...]
    b_im = bim_ref[...]
    bbre_ref[...] = q_re * b_re - q_im * b_im
    bbim_ref[...] = q_re * b_im + q_im * b_re
    ab_ref[0:1, :] = ab_re
    ab_ref[1:2, :] = ab_im


def _s5_prep_call(lam_re, lam_im, log_step, b_re, b_im):
    n = lam_re.shape[1]
    c = b_re.shape[0]
    return pl.pallas_call(
        _s5_prep_kernel,
        out_shape=(jax.ShapeDtypeStruct((2, n), F32),
                   jax.ShapeDtypeStruct((c, n), F32),
                   jax.ShapeDtypeStruct((c, n), F32)),
        name="s5_discretize",
    )(lam_re, lam_im, log_step, b_re, b_im)


def _s5_kernel(x_ref, g_ref, win_ref, ab_ref, bre_ref, bim_ref, cre_ref, cim_ref, d_ref,
               wout_ref, o_ref, u_s, hre_s, him_s, y_s, cre_c, cim_c, *, tl):
    rows = SUBLANES * tl
    n_slabs, slab_ch, slab_st = bre_ref.shape
    half = slab_st // 2

    @pl.when(pl.program_id(0) == 0)
    def _():
        cre_c[...] = jnp.zeros(cre_c.shape, F32)
        cim_c[...] = jnp.zeros(cim_c.shape, F32)

    x = x_ref[...]
    g = g_ref[...]
    xn = _rms(x, g[0:1]).astype(BF16)
    u = _dot(xn, win_ref[...])
    u_s[...] = u
    ub = u.astype(BF16)

    for s in range(n_slabs):
        us = ub[:, s * slab_ch:(s + 1) * slab_ch]
        hre_s[...] = _dot(us, bre_ref[s])
        him_s[...] = _dot(us, bim_ref[s])
        for hf in range(2):
            lanes = slice(hf * half, (hf + 1) * half)
            st0 = s * slab_st + hf * half
            ar = jnp.broadcast_to(ab_ref[0:1, st0:st0 + half], (SUBLANES, half))
            ai = jnp.broadcast_to(ab_ref[1:2, st0:st0 + half], (SUBLANES, half))

            def step(t, carry, lanes=lanes, ar=ar, ai=ai):
                hr, hi = carry
                r0 = pl.multiple_of(t * SUBLANES, SUBLANES)
                nhr = ar * hr - ai * hi + hre_s[pl.ds(r0, SUBLANES), lanes]
                nhi = ar * hi + ai * hr + him_s[pl.ds(r0, SUBLANES), lanes]
                hre_s[pl.ds(r0, SUBLANES), lanes] = nhr
                him_s[pl.ds(r0, SUBLANES), lanes] = nhi
                return nhr, nhi

            hr, hi = lax.fori_loop(0, tl, step, (cre_c[s, :, lanes], cim_c[s, :, lanes]),
                                   unroll=True)
            cre_c[s, :, lanes] = hr
            cim_c[s, :, lanes] = hi
        y_s[:, s * slab_ch:(s + 1) * slab_ch] = (
            _dot(hre_s[...].astype(BF16), cre_ref[s]) - _dot(him_s[...].astype(BF16), cim_ref[s]))

    y = y_s[...] + d_ref[...] * u_s[...]
    y = _gelu_tanh(y).astype(BF16)
    vo = _dot(y, wout_ref[...])
    d_out = vo.shape[1] // 2
    glu = vo[:, :d_out] * _sigmoid(vo[:, d_out:])
    o_ref[...] = x + _rms(glu, g[1:2])


def _s5_call(xf, norm_g, layer, w_in, ab, bre_bd, bim_bd, cre_bd, cim_bd, d_skip, w_out, *, tl=128):
    t, d = xf.shape
    width = w_in.shape[1]
    rows = SUBLANES * tl
    n_slabs, _, slab_st = bre_bd.shape
    return pl.pallas_call(
        functools.partial(_s5_kernel, tl=tl),
        grid=(t // rows,),
        in_specs=[
            pl.BlockSpec((rows, d), lambda i: (i, 0)),
            _const_spec((None, 4, d), (layer, 0, 0)),
            _const_spec(w_in.shape, (0, 0)),
            _const_spec(ab.shape, (0, 0)),
            _const_spec(bre_bd.shape, (0, 0, 0)),
            _const_spec(bim_bd.shape, (0, 0, 0)),
            _const_spec(cre_bd.shape, (0, 0, 0)),
            _const_spec(cim_bd.shape, (0, 0, 0)),
            _const_spec(d_skip.shape, (0, 0)),
            _const_spec(w_out.shape, (0, 0)),
        ],
        out_specs=pl.BlockSpec((rows, d), lambda i: (i, 0)),
        out_shape=jax.ShapeDtypeStruct((t, d), F32),
        scratch_shapes=[
            pltpu.VMEM((rows, width), F32),
            pltpu.VMEM((rows, slab_st), F32),
            pltpu.VMEM((rows, slab_st), F32),
            pltpu.VMEM((rows, width), F32),
            pltpu.VMEM((n_slabs, SUBLANES, slab_st), F32),
            pltpu.VMEM((n_slabs, SUBLANES, slab_st), F32),
        ],
        compiler_params=pltpu.CompilerParams(
            dimension_semantics=("arbitrary",), vmem_limit_bytes=VMEM_LIMIT),
        name=f"s5_{layer}",
    )(xf, norm_g, w_in, ab, bre_bd, bim_bd, cre_bd, cim_bd, d_skip, w_out)


def _cumsum_time(v):
    n = v.shape[0]
    d = SUBLANES
    while d < n:
        v = v + jnp.concatenate([jnp.zeros((d, v.shape[1]), v.dtype), v[:n - d]], axis=0)
        d *= 2
    return v


def _ssd_kernel(x_ref, g_ref, wg_ref, wdt_ref, cw_ref, hp_ref, cp_ref, wout_ref, o_ref,
                xn_s, vn_s, ext, hist, sl, y_sl, csb_s, cst_s, wbt_s, st):
    rows = x_ref.shape[0]
    n_groups = wg_ref.shape[0]
    gw = wout_ref.shape[0] // n_groups
    hpg = gw // SSD_HEAD_DIM
    conv_w = gw + 2 * SSD_STATE
    x_sl = gw // 128
    hist_rows = SUBLANES * (CONV_WIDTH - 1)
    last = SSD_CHUNK - 1

    @pl.when(pl.program_id(0) == 0)
    def _():
        hist[...] = jnp.zeros(hist.shape, F32)
        st[...] = jnp.zeros(st.shape, F32)

    def batch_rows(b):
        return pl.ds(b, SSD_CHUNK, stride=SUBLANES)

    g = g_ref[...]
    hp = hp_ref[...]
    xn_s[...] = _rms(x_ref[...], g[0:1]).astype(BF16)

    dt = _softplus(_dot(xn_s[...], wdt_ref[...]) + hp[0:1])
    cs = _cumsum_time(dt * (-jnp.exp(hp[1:2])))
    cs_end = jnp.broadcast_to(cs[rows - SUBLANES:rows][None], (SSD_CHUNK, SUBLANES, cs.shape[1]))
    cs2 = cs * LOG2_E
    sl[0] = cs2
    sl[1] = cs2 - jnp.log2(dt)
    sl[2] = dt * jnp.exp(cs_end.reshape(cs.shape) - cs)
    for b in range(SUBLANES):
        csb_s[b] = sl[0, batch_rows(b), :]
        cst_s[b] = sl[1, batch_rows(b), :].T
        wbt_s[b] = sl[2, batch_rows(b), :].T

    ii = lax.broadcasted_iota(jnp.int32, (SSD_CHUNK, SSD_CHUNK), 0)
    jj = lax.broadcasted_iota(jnp.int32, (SSD_CHUNK, SSD_CHUNK), 1)
    tri = ii >= jj
    first_head = jj < SSD_HEAD_DIM
    col_head = lax.broadcasted_iota(jnp.int32, (SSD_CHUNK, gw), 1) // SSD_HEAD_DIM

    def conv_to_slabs(gi, pr):
        cw = cw_ref[gi]
        ext[0:hist_rows, :] = hist[gi]
        ext[hist_rows:hist_rows + rows, :] = pr
        acc = cw[CONV_WIDTH:CONV_WIDTH + 1]
        for k in range(CONV_WIDTH):
            acc = acc + cw[k:k + 1] * ext[SUBLANES * k:SUBLANES * k + rows, :]
        hist[gi] = ext[rows:rows + hist_rows, :]
        xbc = _silu_of_twice(acc)
        for j in range(conv_w // 128):
            sl[j] = xbc[:, j * 128:(j + 1) * 128]

    def gate_and_project_out(gi, z):
        y = jnp.concatenate([y_sl[j] for j in range(x_sl)], axis=1)
        v = y * _silu_of_twice(z)
        vn = v * lax.rsqrt(jnp.mean(v * v, axis=-1, keepdims=True) + RMS_EPS)
        k = gi % SSD_OUT_GROUPS
        vn_s[:, k * gw:(k + 1) * gw] = (vn * cp_ref[1:2, gi * gw:(gi + 1) * gw]).astype(BF16)
        if k == SSD_OUT_GROUPS - 1:
            part = _dot(vn_s[...], wout_ref[(gi + 1 - SSD_OUT_GROUPS) * gw:(gi + 1) * gw, :])
            if gi + 1 == SSD_OUT_GROUPS:
                o_ref[...] = part
            else:
                o_ref[...] += part

    for gi in range(n_groups):
        pr = _dot(xn_s[...], wg_ref[gi])
        conv_to_slabs(gi, pr[:, :conv_w])
        z = pr[:, conv_w:conv_w + gw]
        d_skip = cp_ref[0:1, gi * gw:(gi + 1) * gw]

        def per_batch_block(blk, carry, gi=gi, d_skip=d_skip):
            bs = [blk * SSD_BATCH_BLOCK + k for k in range(SSD_BATCH_BLOCK)]
            xg = [jnp.concatenate([sl[j, batch_rows(b), :] for j in range(x_sl)], axis=1)
                  for b in bs]
            bg = [sl[x_sl, batch_rows(b), :] for b in bs]
            cg = [sl[x_sl + 1, batch_rows(b), :].astype(BF16) for b in bs]
            state = [st[b, gi] for b in bs]
            cb = [lax.dot_general(c, bb.astype(BF16), (((1,), (1,)), ((), ())),
                                  preferred_element_type=F32) for c, bb in zip(cg, bg)]
            y_off = [_dot(c, s.astype(BF16)) for c, s in zip(cg, state)]
            cb = [v.astype(BF16) for v in cb]
            bg_t = [bb.T.astype(BF16) for bb in bg]
            lhs, eg = [], []
            for i, b in enumerate(bs):
                cs_b, cst_b, wbt_b = csb_s[b], cst_s[b], wbt_s[b].astype(BF16)
                ms, ws, es = [], [], []
                for r in range(hpg):
                    h = gi * hpg + r
                    cs_col = jnp.broadcast_to(cs_b[:, h:h + 1], (SSD_CHUNK, SSD_CHUNK))
                    lm = jnp.exp2(jnp.where(tri, cs_col - cst_b[h:h + 1, :], -jnp.inf))
                    ms.append(cb[i] * lm.astype(BF16))
                    ws.append(bg_t[i] * wbt_b[h:h + 1, :])
                    es.append(jnp.exp2(cs_col))
                lhs.append(jnp.concatenate(
                    [jnp.concatenate(ms, axis=1), jnp.concatenate(ws, axis=1)], axis=0))
                eg.append(jnp.concatenate(
                    [jnp.where(first_head, es[2 * q], es[2 * q + 1]) for q in range(hpg // 2)],
                    axis=1))
            res = []
            for i in range(SSD_BATCH_BLOCK):
                xgb = xg[i].astype(BF16)
                bd = jnp.concatenate(
                    [jnp.where(col_head == r, xgb, jnp.zeros_like(xgb)) for r in range(hpg)],
                    axis=0)
                res.append(_dot(lhs[i], bd))
            for i, b in enumerate(bs):
                y = res[i][:SSD_CHUNK] + y_off[i] * eg[i] + xg[i] * d_skip
                for j in range(x_sl):
                    y_sl[j, batch_rows(b), :] = y[:, j * 128:(j + 1) * 128]
                st[b, gi] = eg[i][last:last + 1, :] * state[i] + res[i][SSD_CHUNK:]
            return carry

        lax.fori_loop(0, SUBLANES // SSD_BATCH_BLOCK, per_batch_block, 0)
        gate_and_project_out(gi, z)

    o_ref[...] = x_ref[...] + _rms(o_ref[...], g[1:2])


def _ssd_call(xf, norm_g, layer, wg, wdt, cw, hp, cp, w_out):
    t, d = xf.shape
    rows = SUBLANES * SSD_CHUNK
    n_groups = wg.shape[0]
    gw = w_out.shape[0] // n_groups
    conv_w = gw + 2 * SSD_STATE
    hist_rows = SUBLANES * (CONV_WIDTH - 1)
    per_batch = (SUBLANES, SSD_CHUNK, 128)
    return pl.pallas_call(
        _ssd_kernel,
        grid=(t // rows,),
        in_specs=[
            pl.BlockSpec((rows, d), lambda i: (i, 0)),
            _const_spec((None, 4, d), (layer, 0, 0)),
            _const_spec(wg.shape, (0, 0, 0)),
            _const_spec(wdt.shape, (0, 0)),
            _const_spec(cw.shape, (0, 0, 0)),
            _const_spec(hp.shape, (0, 0)),
            _const_spec(cp.shape, (0, 0)),
            _const_spec(w_out.shape, (0, 0)),
        ],
        out_specs=pl.BlockSpec((rows, d), lambda i: (i, 0)),
        out_shape=jax.ShapeDtypeStruct((t, d), F32),
        scratch_shapes=[
            pltpu.VMEM((rows, d), BF16),
            pltpu.VMEM((rows, SSD_OUT_GROUPS * gw), BF16),
            pltpu.VMEM((rows + hist_rows, conv_w), F32),
            pltpu.VMEM((n_groups, hist_rows, conv_w), F32),
            pltpu.VMEM((conv_w // 128, rows, 128), F32),
            pltpu.VMEM((gw // 128, rows, 128), F32),
            pltpu.VMEM(per_batch, F32),
            pltpu.VMEM(per_batch, F32),
            pltpu.VMEM(per_batch, F32),
            pltpu.VMEM((SUBLANES, n_groups, SSD_STATE, gw), F32),
        ],
        compiler_params=pltpu.CompilerParams(
            dimension_semantics=("arbitrary",), vmem_limit_bytes=VMEM_LIMIT),
        name=f"ssd_{layer}",
    )(xf, norm_g, wg, wdt, cw, hp, cp, w_out)


def _block_diag(blocks):
    n, r, c = blocks.shape
    eye = jnp.eye(n, dtype=bool)[:, None, :, None]
    out = jnp.where(eye, blocks[:, :, None, :], jnp.zeros((), blocks.dtype))
    return out.reshape(n * r, n * c)


def _slab_block_diag(blocks, per_slab):
    n, r, c = blocks.shape
    slabs = blocks.reshape(n // per_slab, per_slab, r, c)
    return jax.vmap(_block_diag)(slabs)


def kernel(x, norm_g, mlp_w1, mlp_w2, rg_w_in, rg_conv_w, rg_conv_b, rg_w_a, rg_b_a, rg_w_x, rg_b_x, rg_lam, rg_w_out, ssd_w_in, ssd_conv_w, ssd_conv_b, ssd_dt_bias, ssd_a_log, ssd_d, ssd_norm_g, ssd_w_out, s5_w_in, s5_lam_re, s5_lam_im, s5_log_step, s5_b_re, s5_b_im, s5_c_re, s5_c_im, s5_d, s5_w_out):
    bsz, seq, d = x.shape
    assert bsz == SUBLANES, "time-major mixers put the batch on the 8 sublanes"
    depth = norm_g.shape[0]
    n_mixers = 3

    w1 = mlp_w1.astype(BF16)
    w2 = mlp_w2.astype(BF16)

    for layer in range(depth):
        kind = layer % n_mixers
        j = layer // n_mixers
        if kind == 1:
            inner = ssd_w_out.shape[1]
            conv_dim = ssd_conv_w.shape[2]
            heads = ssd_dt_bias.shape[1]
            gn = SSD_GROUPS * SSD_STATE

            def by_group(m):
                r = m.shape[0]
                parts = [m[:, :inner].reshape(r, SSD_GROUPS, -1),
                         m[:, inner:inner + gn].reshape(r, SSD_GROUPS, -1),
                         m[:, inner + gn:].reshape(r, SSD_GROUPS, -1)]
                return jnp.concatenate(parts, axis=2).transpose(1, 0, 2)

            w_in = ssd_w_in[j].astype(BF16)
            wz = 0.5 * w_in[:, :inner].reshape(d, SSD_GROUPS, -1).transpose(1, 0, 2)
            wg = jnp.concatenate([by_group(w_in[:, inner:inner + conv_dim]), wz], axis=2)
            wdt = jnp.pad(w_in[:, inner + conv_dim:], ((0, 0), (0, 128 - heads)))
            cw = by_group(0.5 * jnp.concatenate(
                [ssd_conv_w[j], ssd_conv_b[j][None], jnp.zeros((3, conv_dim), F32)], axis=0))
            hp = jnp.zeros((SUBLANES, 128), F32)
            hp = hp.at[0, :heads].set(ssd_dt_bias[j]).at[1, :heads].set(ssd_a_log[j])
            cp = jnp.zeros((SUBLANES, inner), F32)
            cp = cp.at[0].set(jnp.repeat(ssd_d[j], SSD_HEAD_DIM)).at[1].set(ssd_norm_g[j])
            x = _ssd_call(x, norm_g, layer, wg, wdt, cw, hp, cp, ssd_w_out[j].astype(BF16))
        else:
            if kind == 0:
                heads = rg_w_a.shape[1]
                per = RG_BLOCK // rg_w_a.shape[2]
                width = rg_w_out.shape[1]
                pvec = jnp.concatenate(
                    [rg_conv_w[j], rg_conv_b[j][None], 0.5 * rg_b_a[j][None],
                     0.5 * rg_b_x[j][None], rg_lam[j][None]], axis=0)
                w_in = rg_w_in[j].astype(BF16)
                w_in = jnp.concatenate([0.5 * w_in[:, :width], w_in[:, width:]], axis=1)
                x = _rg_call(x, norm_g, layer, w_in, pvec,
                             _slab_block_diag(0.5 * rg_w_a[j].astype(BF16), per),
                             _slab_block_diag(0.5 * rg_w_x[j].astype(BF16), per),
                             rg_w_out[j].astype(BF16))
            else:
                groups, states, gch = s5_b_re.shape[1:]
                n = groups * states
                per = S5_SLAB // gch
                ab, bb_re, bb_im = _s5_prep_call(
                    s5_lam_re[j].reshape(1, n), s5_lam_im[j].reshape(1, n),
                    jnp.repeat(s5_log_step[j], states).reshape(1, n),
                    s5_b_re[j].transpose(2, 0, 1).reshape(gch, n),
                    s5_b_im[j].transpose(2, 0, 1).reshape(gch, n))

                def b_slabs(bb):
                    blocks = bb.reshape(gch, groups, states).transpose(1, 0, 2)
                    return _slab_block_diag(blocks.astype(BF16), per)

                def c_slabs(cc):
                    return _slab_block_diag(cc.transpose(0, 2, 1).astype(BF16), per)

                x = _s5_call(x, norm_g, layer, s5_w_in[j].astype(BF16), ab,
                             b_slabs(bb_re), b_slabs(bb_im), c_slabs(s5_c_re[j]),
                             c_slabs(s5_c_im[j]), s5_d[j].reshape(1, -1),
                             s5_w_out[j].astype(BF16))
        x = _mlp_call(x, norm_g, w1, w2, layer, batch_major_out=(layer == depth - 1))

    return x
```

```python
import functools

import jax
import jax.numpy as jnp
from jax import lax
from jax.experimental import pallas as pl
from jax.experimental.pallas import tpu as pltpu

F32 = jnp.float32
BF16 = jnp.bfloat16

RMS_EPS = 1e-6
LOG2_E = 1.4426950408889634
GELU_C0 = 0.7978845608028654
GELU_C1 = 0.044715 * GELU_C0
CONV_WIDTH = 4
RG_C = 8.0
RG_BLOCK = 256
SSD_CHUNK = 128
SSD_HEAD_DIM = 64
SSD_GROUPS = 8
SSD_STATE = 128
SSD_BATCH_BLOCK = 4
SSD_OUT_GROUPS = 2
S5_EIG_CLIP = -1e-4
S5_SLAB = 256

SUBLANES = 8
LANES = 128
VMEM_LIMIT = 60 * 1024 * 1024


def _rms(x, g):
    return x * lax.rsqrt(jnp.mean(x * x, axis=-1, keepdims=True) + RMS_EPS) * g


def _softplus(x):
    return jnp.maximum(x, 0.0) + jnp.log1p(jnp.exp(-jnp.abs(x)))


def _sigmoid(x):
    return 0.5 * jnp.tanh(0.5 * x) + 0.5


def _silu_of_twice(h):
    return h + h * jnp.tanh(h)


def _gelu_tanh(x):
    h = 0.5 * x
    inner = x * (GELU_C0 + GELU_C1 * (x * x))
    return h + h * jnp.tanh(inner)


def _gelu_tanh_of_twice(h):
    inner = h * (2.0 * GELU_C0 + (8.0 * GELU_C1) * (h * h))
    return h + h * jnp.tanh(inner)


def _dot(a, b):
    return jnp.dot(a, b, preferred_element_type=F32)


def _const_spec(shape, index):
    return pl.BlockSpec(shape, lambda *_: index, pipeline_mode=pl.Buffered(1))


def _batch_rows(b, n):
    return pl.ds(b, n, stride=SUBLANES)


def _to_time_major(x_ref, slabs):
    bsz, n, d = x_ref.shape
    for b in range(bsz):
        for j in range(d // LANES):
            slabs[j, _batch_rows(b, n), :] = x_ref[b, :, j * LANES:(j + 1) * LANES]
    return jnp.concatenate([slabs[j] for j in range(d // LANES)], axis=1)


def _store_batch_major(o_ref, v, slabs):
    bsz, n, d = o_ref.shape
    for j in range(d // LANES):
        slabs[j] = v[:, j * LANES:(j + 1) * LANES]
    for b in range(bsz):
        for j in range(d // LANES):
            o_ref[b, :, j * LANES:(j + 1) * LANES] = slabs[j, _batch_rows(b, n), :]


def _mlp_kernel(x_ref, g_ref, w1_ref, w2_ref, o_ref, *slabs, ff_chunk):
    x = x_ref[...]
    g = g_ref[...]
    xn = _rms(x, g[2:3]).astype(BF16)
    d_ff = w1_ref.shape[1]
    acc = jnp.zeros(x.shape, F32)
    for c in range(d_ff // ff_chunk):
        cols = slice(c * ff_chunk, (c + 1) * ff_chunk)
        h = _dot(xn, w1_ref[:, cols])
        h = jnp.square(jnp.maximum(h, 0.0)).astype(BF16)
        acc = acc + _dot(h, w2_ref[cols, :])
    out = x + _rms(acc, g[3:4])
    if slabs:
        _store_batch_major(o_ref, out, slabs[0])
    else:
        o_ref[...] = out


def _mlp_call(xf, norm_g, w1, w2, layer, *, batch_major_out=False, tm=1024, ff_chunk=1024):
    t, d = xf.shape
    d_ff = w1.shape[-1]
    if batch_major_out:
        steps = tm // SUBLANES
        out_spec = pl.BlockSpec((SUBLANES, steps, d), lambda i: (0, i, 0))
        out_shape = jax.ShapeDtypeStruct((SUBLANES, t // SUBLANES, d), F32)
        scratch = [pltpu.VMEM((d // LANES, tm, LANES), F32)]
    else:
        out_spec = pl.BlockSpec((tm, d), lambda i: (i, 0))
        out_shape = jax.ShapeDtypeStruct((t, d), F32)
        scratch = []
    return pl.pallas_call(
        functools.partial(_mlp_kernel, ff_chunk=ff_chunk),
        grid=(t // tm,),
        in_specs=[
            pl.BlockSpec((tm, d), lambda i: (i, 0)),
            _const_spec((None, 4, d), (layer, 0, 0)),
            _const_spec((None, d, d_ff), (layer, 0, 0)),
            _const_spec((None, d_ff, d), (layer, 0, 0)),
        ],
        out_specs=out_spec,
        out_shape=out_shape,
        scratch_shapes=scratch,
        compiler_params=pltpu.CompilerParams(
            dimension_semantics=("parallel",), vmem_limit_bytes=VMEM_LIMIT),
        name=f"mlp_{layer}",
    )(xf, norm_g, w1, w2)


def _rg_kernel(x_ref, g_ref, win_ref, p_ref, wa_ref, wx_ref, wout_ref, o_ref,
               uext, a_s, b_s, gl_s, hc, *slabs, tl):
    rows = SUBLANES * tl
    hist = SUBLANES * (CONV_WIDTH - 1)
    width = wout_ref.shape[0]

    @pl.when(pl.program_id(0) == 0)
    def _():
        uext[0:hist, :] = jnp.zeros((hist, width), F32)
        hc[...] = jnp.zeros(hc.shape, F32)

    x = _to_time_major(x_ref, slabs[0]) if slabs else x_ref[...]
    g = g_ref[...]
    p = p_ref[...]
    xn = _rms(x, g[0:1]).astype(BF16)
    gl_s[...] = _gelu_tanh_of_twice(_dot(xn, win_ref[:, :width]))
    uext[hist:hist + rows, :] = _dot(xn, win_ref[:, width:])

    hu = p[4:5]
    for k in range(CONV_WIDTH):
        hu = hu + p[k:k + 1] * uext[SUBLANES * k:SUBLANES * k + rows, :]
    uext[0:hist, :] = uext[rows:rows + hist, :]

    hub = hu.astype(BF16)
    ra, rx = [], []
    for s in range(width // RG_BLOCK):
        blk = hub[:, s * RG_BLOCK:(s + 1) * RG_BLOCK]
        ra.append(_dot(blk, wa_ref[s]))
        rx.append(_dot(blk, wx_ref[s]))
    ta = jnp.tanh(jnp.concatenate(ra, axis=1) + p[5:6])
    tx = jnp.tanh(jnp.concatenate(rx, axis=1) + p[6:7])
    k = (-0.5 * RG_C * LOG2_E) * _softplus(-p[7:8])
    a = jnp.exp2(k + k * ta)
    a_s[...] = a
    b_s[...] = jnp.sqrt(1.0 - a * a) * (hu + hu * tx)

    def step(t, h):
        r0 = pl.multiple_of(t * SUBLANES, SUBLANES)
        h = a_s[pl.ds(r0, SUBLANES), :] * h + b_s[pl.ds(r0, SUBLANES), :]
        b_s[pl.ds(r0, SUBLANES), :] = h
        return h

    hc[...] = lax.fori_loop(0, tl, step, hc[...], unroll=True)

    y = (b_s[...] * gl_s[...]).astype(BF16)
    o_ref[...] = x + _rms(_dot(y, wout_ref[...]), g[1:2])


def _rg_call(x, norm_g, layer, w_in, pvec, wa_bd, wx_bd, w_out, *, tl=128):
    d = x.shape[-1]
    t = x.size // d
    width = w_out.shape[0]
    rows = SUBLANES * tl
    hist = SUBLANES * (CONV_WIDTH - 1)
    if x.ndim == 3:
        x_spec = pl.BlockSpec((SUBLANES, tl, d), lambda i: (0, i, 0))
        slabs = [pltpu.VMEM((d // LANES, rows, LANES), F32)]
    else:
        x_spec = pl.BlockSpec((rows, d), lambda i: (i, 0))
        slabs = []
    return pl.pallas_call(
        functools.partial(_rg_kernel, tl=tl),
        grid=(t // rows,),
        in_specs=[
            x_spec,
            _const_spec((None, 4, d), (layer, 0, 0)),
            _const_spec(w_in.shape, (0, 0)),
            _const_spec(pvec.shape, (0, 0)),
            _const_spec(wa_bd.shape, (0, 0, 0)),
            _const_spec(wx_bd.shape, (0, 0, 0)),
            _const_spec(w_out.shape, (0, 0)),
        ],
        out_specs=pl.BlockSpec((rows, d), lambda i: (i, 0)),
        out_shape=jax.ShapeDtypeStruct((t, d), F32),
        scratch_shapes=[
            pltpu.VMEM((rows + hist, width), F32),
            pltpu.VMEM((rows, width), F32),
            pltpu.VMEM((rows, width), F32),
            pltpu.VMEM((rows, width), F32),
            pltpu.VMEM((SUBLANES, width), F32),
        ] + slabs,
        compiler_params=pltpu.CompilerParams(
            dimension_semantics=("arbitrary",), vmem_limit_bytes=VMEM_LIMIT),
        name=f"rglru_{layer}",
    )(x, norm_g, w_in, pvec, wa_bd, wx_bd, w_out)


def _s5_prep_kernel(lre_ref, lim_ref, lstep_ref, bre_ref, bim_ref, ab_ref, bbre_ref, bbim_ref):
    lr = jnp.minimum(lre_ref[...], S5_EIG_CLIP)
    li = lim_ref[...]
    step = jnp.exp(lstep_ref[...])
    mag = jnp.exp(lr * step)
    ab_re = mag * jnp.cos(li * step)
    ab_im = mag * jnp.sin(li * step)
    den = lr * lr + li * li
    nr = ab_re - 1.0
    q_re = (nr * lr + ab_im * li) / den
    q_im = (ab_im * lr - nr * li) / den
    b_re = bre_ref[...]
    b_im = bim_ref[...]
    bbre_ref[...] = q_re * b_re - q_im * b_im
    bbim_ref[...] = q_re * b_im + q_im * b_re
    ab_ref[0:1, :] = ab_re
    ab_ref[1:2, :] = ab_im


def _s5_prep_call(lam_re, lam_im, log_step, b_re, b_im):
    n = lam_re.shape[1]
    c = b_re.shape[0]
    return pl.pallas_call(
        _s5_prep_kernel,
        out_shape=(jax.ShapeDtypeStruct((2, n), F32),
                   jax.ShapeDtypeStruct((c, n), F32),
                   jax.ShapeDtypeStruct((c, n), F32)),
        name="s5_discretize",
    )(lam_re, lam_im, log_step, b_re, b_im)


def _s5_kernel(x_ref, g_ref, win_ref, ab_ref, bre_ref, bim_ref, cre_ref, cim_ref, d_ref,
               wout_ref, o_ref, u_s, hre_s, him_s, y_s, cre_c, cim_c, *, tl):
    rows = SUBLANES * tl
    n_slabs, slab_ch, slab_st = bre_ref.shape
    half = slab_st // 2

    @pl.when(pl.program_id(0) == 0)
    def _():
        cre_c[...] = jnp.zeros(cre_c.shape, F32)
        cim_c[...] = jnp.zeros(cim_c.shape, F32)

    x = x_ref[...]
    g = g_ref[...]
    xn = _rms(x, g[0:1]).astype(BF16)
    u = _dot(xn, win_ref[...])
    u_s[...] = u
    ub = u.astype(BF16)

    for s in range(n_slabs):
        us = ub[:, s * slab_ch:(s + 1) * slab_ch]
        hre_s[...] = _dot(us, bre_ref[s])
        him_s[...] = _dot(us, bim_ref[s])
        for hf in range(2):
            lanes = slice(hf * half, (hf + 1) * half)
            st0 = s * slab_st + hf * half
            ar = jnp.broadcast_to(ab_ref[0:1, st0:st0 + half], (SUBLANES, half))
            ai = jnp.broadcast_to(ab_ref[1:2, st0:st0 + half], (SUBLANES, half))

            def step(t, carry, lanes=lanes, ar=ar, ai=ai):
                hr, hi = carry
                r0 = pl.multiple_of(t * SUBLANES, SUBLANES)
                nhr = ar * hr - ai * hi + hre_s[pl.ds(r0, SUBLANES), lanes]
                nhi = ar * hi + ai * hr + him_s[pl.ds(r0, SUBLANES), lanes]
                hre_s[pl.ds(r0, SUBLANES), lanes] = nhr
                him_s[pl.ds(r0, SUBLANES), lanes] = nhi
                return nhr, nhi

            hr, hi = lax.fori_loop(0, tl, step, (cre_c[s, :, lanes], cim_c[s, :, lanes]),
                                   unroll=True)
            cre_c[s, :, lanes] = hr
            cim_c[s, :, lanes] = hi
        y_s[:, s * slab_ch:(s + 1) * slab_ch] = (
            _dot(hre_s[...].astype(BF16), cre_ref[s]) - _dot(him_s[...].astype(BF16), cim_ref[s]))

    y = y_s[...] + d_ref[...] * u_s[...]
    y = _gelu_tanh(y).astype(BF16)
    vo = _dot(y, wout_ref[...])
    d_out = vo.shape[1] // 2
    glu = vo[:, :d_out] * _sigmoid(vo[:, d_out:])
    o_ref[...] = x + _rms(glu, g[1:2])


def _s5_call(xf, norm_g, layer, w_in, ab, bre_bd, bim_bd, cre_bd, cim_bd, d_skip, w_out, *, tl=128):
    t, d = xf.shape
    width = w_in.shape[1]
    rows = SUBLANES * tl
    n_slabs, _, slab_st = bre_bd.shape
    return pl.pallas_call(
        functools.partial(_s5_kernel, tl=tl),
        grid=(t // rows,),
        in_specs=[
            pl.BlockSpec((rows, d), lambda i: (i, 0)),
            _const_spec((None, 4, d), (layer, 0, 0)),
            _const_spec(w_in.shape, (0, 0)),
            _const_spec(ab.shape, (0, 0)),
            _const_spec(bre_bd.shape, (0, 0, 0)),
            _const_spec(bim_bd.shape, (0, 0, 0)),
            _const_spec(cre_bd.shape, (0, 0, 0)),
            _const_spec(cim_bd.shape, (0, 0, 0)),
            _const_spec(d_skip.shape, (0, 0)),
            _const_spec(w_out.shape, (0, 0)),
        ],
        out_specs=pl.BlockSpec((rows, d), lambda i: (i, 0)),
        out_shape=jax.ShapeDtypeStruct((t, d), F32),
        scratch_shapes=[
            pltpu.VMEM((rows, width), F32),
            pltpu.VMEM((rows, slab_st), F32),
            pltpu.VMEM((rows, slab_st), F32),
            pltpu.VMEM((rows, width), F32),
            pltpu.VMEM((n_slabs, SUBLANES, slab_st), F32),
            pltpu.VMEM((n_slabs, SUBLANES, slab_st), F32),
        ],
        compiler_params=pltpu.CompilerParams(
            dimension_semantics=("arbitrary",), vmem_limit_bytes=VMEM_LIMIT),
        name=f"s5_{layer}",
    )(xf, norm_g, w_in, ab, bre_bd, bim_bd, cre_bd, cim_bd, d_skip, w_out)


def _cumsum_time(v):
    n = v.shape[0]
    d = SUBLANES
    while d < n:
        v = v + jnp.concatenate([jnp.zeros((d, v.shape[1]), v.dtype), v[:n - d]], axis=0)
        d *= 2
    return v


def _ssd_kernel(x_ref, g_ref, wg_ref, wdt_ref, cw_ref, hp_ref, cp_ref, wout_ref, o_ref,
                xn_s, vn_s, ext, hist, sl, y_sl, csb_s, cst_s, wbt_s, st):
    rows = x_ref.shape[0]
    n_groups = wg_ref.shape[0]
    gw = wout_ref.shape[0] // n_groups
    hpg = gw // SSD_HEAD_DIM
    conv_w = gw + 2 * SSD_STATE
    x_sl = gw // 128
    hist_rows = SUBLANES * (CONV_WIDTH - 1)
    last = SSD_CHUNK - 1

    @pl.when(pl.program_id(0) == 0)
    def _():
        hist[...] = jnp.zeros(hist.shape, F32)
        st[...] = jnp.zeros(st.shape, F32)

    def batch_rows(b):
        return pl.ds(b, SSD_CHUNK, stride=SUBLANES)

    g = g_ref[...]
    hp = hp_ref[...]
    xn_s[...] = _rms(x_ref[...], g[0:1]).astype(BF16)

    dt = _softplus(_dot(xn_s[...], wdt_ref[...]) + hp[0:1])
    cs = _cumsum_time(dt * (-jnp.exp(hp[1:2])))
    cs_end = jnp.broadcast_to(cs[rows - SUBLANES:rows][None], (SSD_CHUNK, SUBLANES, cs.shape[1]))
    cs2 = cs * LOG2_E
    sl[0] = cs2
    sl[1] = cs2 - jnp.log2(dt)
    sl[2] = dt * jnp.exp(cs_end.reshape(cs.shape) - cs)
    for b in range(SUBLANES):
        csb_s[b] = sl[0, batch_rows(b), :]
        cst_s[b] = sl[1, batch_rows(b), :].T
        wbt_s[b] = sl[2, batch_rows(b), :].T

    ii = lax.broadcasted_iota(jnp.int32, (SSD_CHUNK, SSD_CHUNK), 0)
    jj = lax.broadcasted_iota(jnp.int32, (SSD_CHUNK, SSD_CHUNK), 1)
    tri = ii >= jj
    first_head = jj < SSD_HEAD_DIM
    col_head = lax.broadcasted_iota(jnp.int32, (SSD_CHUNK, gw), 1) // SSD_HEAD_DIM

    def conv_to_slabs(gi, pr):
        cw = cw_ref[gi]
        ext[0:hist_rows, :] = hist[gi]
        ext[hist_rows:hist_rows + rows, :] = pr
        acc = cw[CONV_WIDTH:CONV_WIDTH + 1]
        for k in range(CONV_WIDTH):
            acc = acc + cw[k:k + 1] * ext[SUBLANES * k:SUBLANES * k + rows, :]
        hist[gi] = ext[rows:rows + hist_rows, :]
        xbc = _silu_of_twice(acc)
        for j in range(conv_w // 128):
            sl[j] = xbc[:, j * 128:(j + 1) * 128]

    def gate_and_project_out(gi, z):
        y = jnp.concatenate([y_sl[j] for j in range(x_sl)], axis=1)
        v = y * _silu_of_twice(z)
        vn = v * lax.rsqrt(jnp.mean(v * v, axis=-1, keepdims=True) + RMS_EPS)
        k = gi % SSD_OUT_GROUPS
        vn_s[:, k * gw:(k + 1) * gw] = (vn * cp_ref[1:2, gi * gw:(gi + 1) * gw]).astype(BF16)
        if k == SSD_OUT_GROUPS - 1:
            part = _dot(vn_s[...], wout_ref[(gi + 1 - SSD_OUT_GROUPS) * gw:(gi + 1) * gw, :])
            if gi + 1 == SSD_OUT_GROUPS:
                o_ref[...] = part
            else:
                o_ref[...] += part

    for gi in range(n_groups):
        pr = _dot(xn_s[...], wg_ref[gi])
        conv_to_slabs(gi, pr[:, :conv_w])
        z = pr[:, conv_w:conv_w + gw]
        d_skip = cp_ref[0:1, gi * gw:(gi + 1) * gw]

        def per_batch_block(blk, carry, gi=gi, d_skip=d_skip):
            bs = [blk * SSD_BATCH_BLOCK + k for k in range(SSD_BATCH_BLOCK)]
            xg = [jnp.concatenate([sl[j, batch_rows(b), :] for j in range(x_sl)], axis=1)
                  for b in bs]
            bg = [sl[x_sl, batch_rows(b), :] for b in bs]
            cg = [sl[x_sl + 1, batch_rows(b), :].astype(BF16) for b in bs]
            state = [st[b, gi] for b in bs]
            cb = [lax.dot_general(c, bb.astype(BF16), (((1,), (1,)), ((), ())),
                                  preferred_element_type=F32) for c, bb in zip(cg, bg)]
            y_off = [_dot(c, s.astype(BF16)) for c, s in zip(cg, state)]
            cb = [v.astype(BF16) for v in cb]
            bg_t = [bb.T.astype(BF16) for bb in bg]
            lhs, eg = [], []
            for i, b in enumerate(bs):
                cs_b, cst_b, wbt_b = csb_s[b], cst_s[b], wbt_s[b].astype(BF16)
                ms, ws, es = [], [], []
                for r in range(hpg):
                    h = gi * hpg + r
                    cs_col = jnp.broadcast_to(cs_b[:, h:h + 1], (SSD_CHUNK, SSD_CHUNK))
                    lm = jnp.exp2(jnp.where(tri, cs_col - cst_b[h:h + 1, :], -jnp.inf))
                    ms.append(cb[i] * lm.astype(BF16))
                    ws.append(bg_t[i] * wbt_b[h:h + 1, :])
                    es.append(jnp.exp2(cs_col))
                lhs.append(jnp.concatenate(
                    [jnp.concatenate(ms, axis=1), jnp.concatenate(ws, axis=1)], axis=0))
                eg.append(jnp.concatenate(
                    [jnp.where(first_head, es[2 * q], es[2 * q + 1]) for q in range(hpg // 2)],
                    axis=1))
            res = []
            for i in range(SSD_BATCH_BLOCK):
                xgb = xg[i].astype(BF16)
                bd = jnp.concatenate(
                    [jnp.where(col_head == r, xgb, jnp.zeros_like(xgb)) for r in range(hpg)],
                    axis=0)
                res.append(_dot(lhs[i], bd))
            for i, b in enumerate(bs):
                y = res[i][:SSD_CHUNK] + y_off[i] * eg[i] + xg[i] * d_skip
                for j in range(x_sl):
                    y_sl[j, batch_rows(b), :] = y[:, j * 128:(j + 1) * 128]
                st[b, gi] = eg[i][last:last + 1, :] * state[i] + res[i][SSD_CHUNK:]
            return carry

        lax.fori_loop(0, SUBLANES // SSD_BATCH_BLOCK, per_batch_block, 0)
        gate_and_project_out(gi, z)

    o_ref[...] = x_ref[...] + _rms(o_ref[...], g[1:2])


def _ssd_call(xf, norm_g, layer, wg, wdt, cw, hp, cp, w_out):
    t, d = xf.shape
    rows = SUBLANES * SSD_CHUNK
    n_groups = wg.shape[0]
    gw = w_out.shape[0] // n_groups
    conv_w = gw + 2 * SSD_STATE
    hist_rows = SUBLANES * (CONV_WIDTH - 1)
    per_batch = (SUBLANES, SSD_CHUNK, 128)
    return pl.pallas_call(
        _ssd_kernel,
        grid=(t // rows,),
        in_specs=[
            pl.BlockSpec((rows, d), lambda i: (i, 0)),
            _const_spec((None, 4, d), (layer, 0, 0)),
            _const_spec(wg.shape, (0, 0, 0)),
            _const_spec(wdt.shape, (0, 0)),
            _const_spec(cw.shape, (0, 0, 0)),
            _const_spec(hp.shape, (0, 0)),
            _const_spec(cp.shape, (0, 0)),
            _const_spec(w_out.shape, (0, 0)),
        ],
        out_specs=pl.BlockSpec((rows, d), lambda i: (i, 0)),
        out_shape=jax.ShapeDtypeStruct((t, d), F32),
        scratch_shapes=[
            pltpu.VMEM((rows, d), BF16),
            pltpu.VMEM((rows, SSD_OUT_GROUPS * gw), BF16),
            pltpu.VMEM((rows + hist_rows, conv_w), F32),
            pltpu.VMEM((n_groups, hist_rows, conv_w), F32),
            pltpu.VMEM((conv_w // 128, rows, 128), F32),
            pltpu.VMEM((gw // 128, rows, 128), F32),
            pltpu.VMEM(per_batch, F32),
            pltpu.VMEM(per_batch, F32),
            pltpu.VMEM(per_batch, F32),
            pltpu.VMEM((SUBLANES, n_groups, SSD_STATE, gw), F32),
        ],
        compiler_params=pltpu.CompilerParams(
            dimension_semantics=("arbitrary",), vmem_limit_bytes=VMEM_LIMIT),
        name=f"ssd_{layer}",
    )(xf, norm_g, wg, wdt, cw, hp, cp, w_out)


def _block_diag(blocks):
    n, r, c = blocks.shape
    eye = jnp.eye(n, dtype=bool)[:, None, :, None]
    out = jnp.where(eye, blocks[:, :, None, :], jnp.zeros((), blocks.dtype))
    return out.reshape(n * r, n * c)


def _slab_block_diag(blocks, per_slab):
    n, r, c = blocks.shape
    slabs = blocks.reshape(n // per_slab, per_slab, r, c)
    return jax.vmap(_block_diag)(slabs)


def kernel(x, norm_g, mlp_w1, mlp_w2, rg_w_in, rg_conv_w, rg_conv_b, rg_w_a, rg_b_a, rg_w_x, rg_b_x, rg_lam, rg_w_out, ssd_w_in, ssd_conv_w, ssd_conv_b, ssd_dt_bias, ssd_a_log, ssd_d, ssd_norm_g, ssd_w_out, s5_w_in, s5_lam_re, s5_lam_im, s5_log_step, s5_b_re, s5_b_im, s5_c_re, s5_c_im, s5_d, s5_w_out):
    bsz, seq, d = x.shape
    assert bsz == SUBLANES, "time-major mixers put the batch on the 8 sublanes"
    depth = norm_g.shape[0]
    n_mixers = 3

    w1 = mlp_w1.astype(BF16)
    w2 = mlp_w2.astype(BF16)

    for layer in range(depth):
        kind = layer % n_mixers
        j = layer // n_mixers
        if kind == 1:
            inner = ssd_w_out.shape[1]
            conv_dim = ssd_conv_w.shape[2]
            heads = ssd_dt_bias.shape[1]
            gn = SSD_GROUPS * SSD_STATE

            def by_group(m):
                r = m.shape[0]
                parts = [m[:, :inner].reshape(r, SSD_GROUPS, -1),
                         m[:, inner:inner + gn].reshape(r, SSD_GROUPS, -1),
                         m[:, inner + gn:].reshape(r, SSD_GROUPS, -1)]
                return jnp.concatenate(parts, axis=2).transpose(1, 0, 2)

            w_in = ssd_w_in[j].astype(BF16)
            wz = 0.5 * w_in[:, :inner].reshape(d, SSD_GROUPS, -1).transpose(1, 0, 2)
            wg = jnp.concatenate([by_group(w_in[:, inner:inner + conv_dim]), wz], axis=2)
            wdt = jnp.pad(w_in[:, inner + conv_dim:], ((0, 0), (0, 128 - heads)))
            cw = by_group(0.5 * jnp.concatenate(
                [ssd_conv_w[j], ssd_conv_b[j][None], jnp.zeros((3, conv_dim), F32)], axis=0))
            hp = jnp.zeros((SUBLANES, 128), F32)
            hp = hp.at[0, :heads].set(ssd_dt_bias[j]).at[1, :heads].set(ssd_a_log[j])
            cp = jnp.zeros((SUBLANES, inner), F32)
            cp = cp.at[0].set(jnp.repeat(ssd_d[j], SSD_HEAD_DIM)).at[1].set(ssd_norm_g[j])
            x = _ssd_call(x, norm_g, layer, wg, wdt, cw, hp, cp, ssd_w_out[j].astype(BF16))
        else:
            if kind == 0:
                heads = rg_w_a.shape[1]
                per = RG_BLOCK // rg_w_a.shape[2]
                width = rg_w_out.shape[1]
                pvec = jnp.concatenate(
                    [0.5 * rg_conv_w[j], 0.5 * rg_conv_b[j][None], 0.5 * rg_b_a[j][None],
                     0.5 * rg_b_x[j][None], rg_lam[j][None]], axis=0)
                w_in = rg_w_in[j].astype(BF16)
                w_in = jnp.concatenate([0.5 * w_in[:, :width], w_in[:, width:]], axis=1)
                x = _rg_call(x, norm_g, layer, w_in, pvec,
                             _slab_block_diag(rg_w_a[j].astype(BF16), per),
                             _slab_block_diag(rg_w_x[j].astype(BF16), per),
                             rg_w_out[j].astype(BF16))
            else:
                groups, states, gch = s5_b_re.shape[1:]
                n = groups * states
                per = S5_SLAB // gch
                ab, bb_re, bb_im = _s5_prep_call(
                    s5_lam_re[j].reshape(1, n), s5_lam_im[j].reshape(1, n),
                    jnp.repeat(s5_log_step[j], states).reshape(1, n),
                    s5_b_re[j].transpose(2, 0, 1).reshape(gch, n),
                    s5_b_im[j].transpose(2, 0, 1).reshape(gch, n))

                def b_slabs(bb):
                    blocks = bb.reshape(gch, groups, states).transpose(1, 0, 2)
                    return _slab_block_diag(blocks.astype(BF16), per)

                def c_slabs(cc):
                    return _slab_block_diag(cc.transpose(0, 2, 1).astype(BF16), per)

                x = _s5_call(x, norm_g, layer, s5_w_in[j].astype(BF16), ab,
                             b_slabs(bb_re), b_slabs(bb_im), c_slabs(s5_c_re[j]),
                             c_slabs(s5_c_im[j]), s5_d[j].reshape(1, -1),
                             s5_w_out[j].astype(BF16))
        x = _mlp_call(x, norm_g, w1, w2, layer, batch_major_out=(layer == depth - 1))

    return x
```
